```python
import functools
import jax, jax.numpy as jnp
from jax import lax
import numpy as np

D_MODEL = 1024
BATCH = 4
SEQ = 8192
DEPTH = 1
DEC_BATCH = 32
DEC_SEQ = 32
PAST_LEN = 1024

CHUNK = 64
D_CONV = 512
CONV_WIDTH = 31
N_Q_HEADS = 8
N_KV_HEADS = 2
HEAD_DIM = 64
GROUP = N_Q_HEADS // N_KV_HEADS
D_ATTN = N_Q_HEADS * HEAD_DIM
D_KV = N_KV_HEADS * HEAD_DIM
WINDOW = 128
WIN_CHUNKS = WINDOW // CHUNK
D_MIX = D_CONV + D_ATTN
D_IN = 2 * D_CONV + D_ATTN + 2 * D_KV
D_FF = 2816
FFN_CONV_WIDTH = 3
EPS = 1e-6
NEG_INF = -1e30

kernel_name = 'hymba_conformer_swa_sink_convffn_step'


def rmsnorm(x, g):
    x32 = x.astype(jnp.float32)
    y = x32 * lax.rsqrt(jnp.mean(x32 * x32, axis=-1, keepdims=True) + EPS)
    return y.astype(x.dtype) * g


def layernorm(x, g, b):
    x32 = x.astype(jnp.float32)
    mu = jnp.mean(x32, axis=-1, keepdims=True)
    xc = x32 - mu
    y = xc * lax.rsqrt(jnp.mean(xc * xc, axis=-1, keepdims=True) + EPS)
    return y.astype(x.dtype) * g + b


def causal_dwconv(x, hist, w, b):
    width = w.shape[0]
    xp = jnp.concatenate([hist.astype(x.dtype), x], axis=1)
    y = lax.conv_general_dilated(xp, w[:, None, :].astype(x.dtype), window_strides=(1,), padding='VALID',
                                 dimension_numbers=('NWC', 'WIO', 'NWC'), feature_group_count=x.shape[-1])
    return y + b, xp[:, -(width - 1):]


def sink_attention(q, k, v, sinks, key_valid):
    s = jnp.einsum('bcqhgd,bckhd->bchgqk', q, k).astype(jnp.float32) * (HEAD_DIM ** -0.5)
    if key_valid is not None:
        s = jnp.where(key_valid[None, :, None, None, None, :], s, NEG_INF)
    sink = sinks.astype(jnp.float32).reshape(1, 1, N_KV_HEADS, GROUP, 1, 1)
    m = jnp.maximum(jnp.max(s, axis=-1, keepdims=True), sink)
    p = jnp.exp(s - m)
    denom = jnp.sum(p, axis=-1, keepdims=True) + jnp.exp(sink - m)
    return jnp.einsum('bchgqk,bckhd->bcqhgd', (p / denom).astype(v.dtype), v)


def attn_prompt(q, k, v, sinks):
    b, s = q.shape[0], q.shape[1]
    n_c = s // CHUNK
    qc = q.reshape(b, n_c, CHUNK, N_KV_HEADS, GROUP, HEAD_DIM)
    pad = ((0, 0), (WIN_CHUNKS * CHUNK, 0), (0, 0), (0, 0))
    kp = jnp.pad(k, pad).reshape(b, n_c + WIN_CHUNKS, CHUNK, N_KV_HEADS, HEAD_DIM)
    vp = jnp.pad(v, pad).reshape(b, n_c + WIN_CHUNKS, CHUNK, N_KV_HEADS, HEAD_DIM)
    kb = jnp.concatenate([kp[:, i:i + n_c] for i in range(WIN_CHUNKS + 1)], axis=2)
    vb = jnp.concatenate([vp[:, i:i + n_c] for i in range(WIN_CHUNKS + 1)], axis=2)
    key_chunk = jnp.arange(n_c)[:, None] - WIN_CHUNKS + (jnp.arange((WIN_CHUNKS + 1) * CHUNK) // CHUNK)[None, :]
    o = sink_attention(qc, kb, vb, sinks, key_chunk >= 0)
    return o.reshape(b, s, D_ATTN), k[:, -WINDOW:], v[:, -WINDOW:]


def attn_sample(q, k, v, sinks, k_hist, v_hist):
    b, t = q.shape[0], q.shape[1]
    k_all = jnp.concatenate([k_hist.astype(k.dtype), k], axis=1)
    v_all = jnp.concatenate([v_hist.astype(v.dtype), v], axis=1)
    o = sink_attention(q[:, None], k_all[:, None], v_all[:, None], sinks, None)
    return o.reshape(b, t, D_ATTN), k_all[:, -WINDOW:], v_all[:, -WINDOW:]


def layer(x, conv_hist, ffn_hist, attn_fn, g_attn_norm, w_in, conv_w, conv_b, conv_ln_g, conv_ln_b, sinks,
          g_out_conv, g_out_attn, w_out, g_ffn_norm, w_up, ffn_conv_w, ffn_conv_b, w_down):
    b, t = x.shape[0], x.shape[1]
    h = rmsnorm(x, g_attn_norm)
    u = h @ w_in
    ga, gb, q, k, v = jnp.split(u, [D_CONV, 2 * D_CONV, 2 * D_CONV + D_ATTN, 2 * D_CONV + D_ATTN + D_KV], axis=-1)
    glu = ga * jax.nn.sigmoid(gb)
    cy, conv_new = causal_dwconv(glu, conv_hist, conv_w, conv_b)
    cy = jax.nn.silu(layernorm(cy, conv_ln_g, conv_ln_b))
    q = q.reshape(b, t, N_KV_HEADS, GROUP, HEAD_DIM)
    k = k.reshape(b, t, N_KV_HEADS, HEAD_DIM)
    v = v.reshape(b, t, N_KV_HEADS, HEAD_DIM)
    ao, k_new, v_new = attn_fn(q, k, v, sinks)
    mix = jnp.concatenate([rmsnorm(cy, g_out_conv), rmsnorm(ao, g_out_attn)], axis=-1)
    x = x + mix @ w_out
    h2 = rmsnorm(x, g_ffn_norm)
    gate, val = jnp.split(h2 @ w_up, 2, axis=-1)
    gc, ffn_new = causal_dwconv(gate, ffn_hist, ffn_conv_w, ffn_conv_b)
    x = x + (jax.nn.gelu(gc) * val) @ w_down
    return x, conv_new, k_new, v_new, ffn_new


def setup_inputs(seed: int = 0) -> dict:
    key = jax.random.key(seed)
    ks = jax.random.split(key, 24)
    f32 = jnp.float32
    nrm = lambda k, shp, sc: jax.random.normal(k, shp, f32) * sc
    L = DEPTH
    return {
        'x_prompt': nrm(ks[0], (BATCH, SEQ, D_MODEL), 1.0),
        'x_sample': nrm(ks[1], (DEC_BATCH, DEC_SEQ, D_MODEL), 1.0),
        'cache_conv': nrm(ks[2], (L, DEC_BATCH, CONV_WIDTH - 1, D_CONV), 0.5),
        'cache_k': nrm(ks[3], (L, DEC_BATCH, WINDOW, N_KV_HEADS, HEAD_DIM), 1.0),
        'cache_v': nrm(ks[4], (L, DEC_BATCH, WINDOW, N_KV_HEADS, HEAD_DIM), 1.0),
        'cache_ffn_conv': nrm(ks[5], (L, DEC_BATCH, FFN_CONV_WIDTH - 1, D_FF), 1.0),
        'g_attn_norm': 1.0 + nrm(ks[6], (L, D_MODEL), 0.01),
        'w_in': nrm(ks[7], (L, D_MODEL, D_IN), D_MODEL ** -0.5),
        'conv_w': nrm(ks[8], (L, CONV_WIDTH, D_CONV), CONV_WIDTH ** -0.5),
        'conv_b': nrm(ks[9], (L, D_CONV), 0.01),
        'conv_ln_g': 1.0 + nrm(ks[10], (L, D_CONV), 0.01),
        'conv_ln_b': nrm(ks[11], (L, D_CONV), 0.01),
        'sinks': nrm(ks[12], (L, N_Q_HEADS), 0.5),
        'g_out_conv': 1.0 + nrm(ks[13], (L, D_CONV), 0.01),
        'g_out_attn': 1.0 + nrm(ks[14], (L, D_ATTN), 0.01),
        'w_out': nrm(ks[15], (L, D_MIX, D_MODEL), D_MIX ** -0.5),
        'g_ffn_norm': 1.0 + nrm(ks[16], (L, D_MODEL), 0.01),
        'w_up': nrm(ks[17], (L, D_MODEL, 2 * D_FF), D_MODEL ** -0.5),
        'ffn_conv_w': nrm(ks[18], (L, FFN_CONV_WIDTH, D_FF), FFN_CONV_WIDTH ** -0.5),
        'ffn_conv_b': nrm(ks[19], (L, D_FF), 0.01),
        'w_down': nrm(ks[20], (L, D_FF, D_MODEL), D_FF ** -0.5),
        'g_final': 1.0 + nrm(ks[21], (D_MODEL,), 0.01),
    }


def reference(x_prompt, x_sample, cache_conv, cache_k, cache_v, cache_ffn_conv, g_attn_norm, w_in, conv_w,
              conv_b, conv_ln_g, conv_ln_b, sinks, g_out_conv, g_out_attn, w_out, g_ffn_norm, w_up,
              ffn_conv_w, ffn_conv_b, w_down, g_final):
    xp, xs = x_prompt, x_sample
    bp = xp.shape[0]
    pc, pk, pv, pf, sc, sk, sv, sf = [], [], [], [], [], [], [], []
    for l in range(DEPTH):
        w = (g_attn_norm[l], w_in[l], conv_w[l], conv_b[l], conv_ln_g[l], conv_ln_b[l], sinks[l],
             g_out_conv[l], g_out_attn[l], w_out[l], g_ffn_norm[l], w_up[l], ffn_conv_w[l], ffn_conv_b[l], w_down[l])
        xp, c_new, k_new, v_new, f_new = layer(
            xp, jnp.zeros((bp, CONV_WIDTH - 1, D_CONV), xp.dtype), jnp.zeros((bp, FFN_CONV_WIDTH - 1, D_FF), xp.dtype),
            attn_prompt, *w)
        pc.append(c_new); pk.append(k_new); pv.append(v_new); pf.append(f_new)
        xs, c_new, k_new, v_new, f_new = layer(
            xs, cache_conv[l], cache_ffn_conv[l],
            functools.partial(attn_sample, k_hist=cache_k[l], v_hist=cache_v[l]), *w)
        sc.append(c_new); sk.append(k_new); sv.append(v_new); sf.append(f_new)
    y_prompt = rmsnorm(xp, g_final)
    y_sample = rmsnorm(xs, g_final)
    return (y_prompt, y_sample, jnp.stack(pc), jnp.stack(pk), jnp.stack(pv), jnp.stack(pf),
            jnp.stack(sc), jnp.stack(sk), jnp.stack(sv), jnp.stack(sf))
```

```python
import functools
import math

import jax
import jax.numpy as jnp
from jax import lax
from jax.experimental import pallas as pl
from jax.experimental.pallas import tpu as pltpu

D_MODEL = 1024
CHUNK = 64
D_CONV = 512
CONV_WIDTH = 31
N_Q_HEADS = 8
N_KV_HEADS = 2
HEAD_DIM = 64
D_ATTN = N_Q_HEADS * HEAD_DIM
D_KV = N_KV_HEADS * HEAD_DIM
WINDOW = 128
D_MIX = D_CONV + D_ATTN
D_IN = 2 * D_CONV + D_ATTN + 2 * D_KV
D_FF = 2816
FFN_CONV_WIDTH = 3
EPS = 1e-6
NEG_INF = -1e30
SCALE = HEAD_DIM ** -0.5

LANES = 128
KEY_PAD = 256
CONV_PAD = 32
FFN_PAD = 8
FFN_COLS = D_FF // 2
N_FFN_PASSES = D_FF // FFN_COLS
FFN_ROWS = 16
CONV_ROWS = 32
PROMPT_TILE = 512
SAMPLE_STREAMS = 8
VMEM_LIMIT_BYTES = 58 * 1024 * 1024

BF16 = jnp.bfloat16
F32 = jnp.float32


def _dot(a, b):
    return jnp.dot(a, b, preferred_element_type=F32)


def _dot_t(a, b):
    return lax.dot_general(a, b, (((1,), (1,)), ((), ())), preferred_element_type=F32)


def _rms(x, g):
    return x * lax.rsqrt(jnp.mean(x * x, axis=-1, keepdims=True) + EPS) * g


def _gelu_tanh(x):
    inner = math.sqrt(2.0 / math.pi) * (x + 0.044715 * (x * x * x))
    return x * (0.5 * (1.0 + jnp.tanh(inner)))


def _block_diag_parts(a):
    lo = lax.broadcasted_iota(jnp.int32, a.shape, 1) < HEAD_DIM
    swapped = pltpu.roll(a, HEAD_DIM, 1)
    zero = jnp.zeros_like(a)
    parts = (jnp.where(lo, a, zero), jnp.where(lo, zero, swapped),
             jnp.where(lo, swapped, zero), jnp.where(lo, zero, a))
    return [p.astype(BF16) for p in parts]


def _conv_mix_block(glu_view, r0, rows, cw_ref, cb_ref, lng_ref, lnb_ref, goc_ref):
    first = CONV_PAD - (CONV_WIDTH - 1)
    window = glu_view[pl.ds(r0, CONV_PAD + rows), :]
    acc = jnp.broadcast_to(cb_ref[...], (rows, D_CONV))
    for j in range(CONV_WIDTH):
        acc = acc + cw_ref[j:j + 1, :] * window[first + j:first + j + rows, :]
    mu = jnp.mean(acc, axis=-1, keepdims=True)
    xc = acc - mu
    y = xc * lax.rsqrt(jnp.mean(xc * xc, axis=-1, keepdims=True) + EPS) * lng_ref[...] + lnb_ref[...]
    cy = y * jax.nn.sigmoid(y)
    return _rms(cy, goc_ref[...])


def _attn_block(q_ref, qrow, rows, kget, vget, valid, sinks_ref, out_ref):
    for pair in range(N_Q_HEADS // 2):
        kv_head = pair // 2
        q = q_ref[pl.ds(qrow, rows), LANES * pair:LANES * (pair + 1)]
        out = None
        for half in range(2):
            s = _dot_t(q, kget(2 * kv_head + half)) * SCALE
            s = jnp.where(valid, s, NEG_INF)
            sink = sinks_ref[2 * pair + half]
            m = jnp.maximum(jnp.max(s, axis=-1, keepdims=True), sink)
            e = jnp.exp(s - m)
            denom = jnp.sum(e, axis=-1, keepdims=True) + jnp.exp(sink - m)
            pv = _dot(e.astype(BF16), vget(2 * kv_head + half)) * (1.0 / denom)
            out = pv if out is None else out + pv
        out_ref[pl.ds(qrow, rows), LANES * pair:LANES * (pair + 1)] = out


def _ffn_block(gate_view, r0, val_ref, act_ref, row, cols, fcw_ref, fcb_ref):
    first = FFN_PAD - (FFN_CONV_WIDTH - 1)
    window = gate_view[pl.ds(r0, FFN_PAD + FFN_ROWS), :]
    gc = jnp.broadcast_to(fcb_ref[:, cols], (FFN_ROWS, FFN_COLS))
    for j in range(FFN_CONV_WIDTH):
        gc = gc + fcw_ref[j:j + 1, cols] * window[first + j:first + j + FFN_ROWS, :]
    act = _gelu_tanh(gc) * val_ref[pl.ds(row, FFN_ROWS), :]
    act_ref[pl.ds(row, FFN_ROWS), :] = act.astype(BF16)


def _prompt_kernel(x_ref, ga_ref, w_in_ref, cw_ref, cb_ref, lng_ref, lnb_ref, sinks_ref, goc_ref, goa_ref,
                   w_out_ref, gffn_ref, w_up_ref, fcw_ref, fcb_ref, w_down_ref, gfin_ref,
                   y_ref, convn_ref, kn_ref, vn_ref, ffnn_ref,
                   glu_buf, q_buf, kbd, vbd, ao_buf, mix_buf, x1_buf, h2_buf, gate_buf, ghist, val_buf, act_buf):
    t = pl.program_id(1)
    last = pl.num_programs(1) - 1
    tile = x_ref.shape[1]
    chunks = tile // CHUNK

    @pl.when(t == 0)
    def _():
        glu_buf[0:CONV_PAD, :] = jnp.zeros((CONV_PAD, D_CONV), F32)
        ghist[...] = jnp.zeros(ghist.shape, F32)
        for buf in (kbd, vbd):
            buf[:, 0:WINDOW, :] = jnp.zeros((4, WINDOW, LANES), BF16)
            buf[:, WINDOW + tile:, :] = jnp.zeros((4, CHUNK, LANES), BF16)

    hb = _rms(x_ref[0], ga_ref[...]).astype(BF16)
    ab = _dot(hb, w_in_ref[:, 0:2 * D_CONV])
    glu_buf[CONV_PAD:, :] = ab[:, 0:D_CONV] * jax.nn.sigmoid(ab[:, D_CONV:])
    q_buf[...] = _dot(hb, w_in_ref[:, 2 * D_CONV:2 * D_CONV + D_ATTN]).astype(BF16)
    kv = _dot(hb, w_in_ref[:, 2 * D_CONV + D_ATTN:])
    k = kv[:, 0:D_KV]
    v = kv[:, D_KV:]
    for i, part in enumerate(_block_diag_parts(k)):
        kbd[i, WINDOW:WINDOW + tile, :] = part
    for i, part in enumerate(_block_diag_parts(v)):
        vbd[i, WINDOW:WINDOW + tile, :] = part

    @pl.when(t == last)
    def _():
        kn_ref[0] = k[tile - WINDOW:, :]
        vn_ref[0] = v[tile - WINDOW:, :]
        convn_ref[0] = glu_buf[CONV_PAD + tile - (CONV_WIDTH - 1):, :]

    def conv_body(r, carry):
        r0 = pl.multiple_of(r * CONV_ROWS, CONV_ROWS)
        n = _conv_mix_block(glu_buf, r0, CONV_ROWS, cw_ref, cb_ref, lng_ref, lnb_ref, goc_ref)
        mix_buf[pl.ds(r0, CONV_ROWS), 0:D_CONV] = n.astype(BF16)
        return carry

    lax.fori_loop(0, tile // CONV_ROWS, conv_body, 0)
    glu_buf[0:CONV_PAD, :] = glu_buf[tile:, :]

    col = lax.broadcasted_iota(jnp.int32, (CHUNK, KEY_PAD), 1)

    def attn_body(c, carry):
        r0 = pl.multiple_of(c * CHUNK, CHUNK)
        first_valid = jnp.maximum(0, (2 - (t * chunks + c)) * CHUNK)
        valid = (col >= first_valid) & (col < 3 * CHUNK)
        _attn_block(q_buf, r0, CHUNK,
                    lambda i: kbd[i, pl.ds(r0, KEY_PAD), :],
                    lambda i: vbd[i, pl.ds(r0, KEY_PAD), :],
                    valid, sinks_ref, ao_buf)
        return carry

    lax.fori_loop(0, chunks, attn_body, 0)
    for buf in (kbd, vbd):
        buf[:, 0:WINDOW, :] = buf[:, tile:tile + WINDOW, :]
    mix_buf[:, D_CONV:] = _rms(ao_buf[...], goa_ref[...]).astype(BF16)

    x1 = x_ref[0] + _dot(mix_buf[...], w_out_ref[...])
    x1_buf[...] = x1
    h2_buf[...] = _rms(x1, gffn_ref[...]).astype(BF16)

    for c in range(N_FFN_PASSES):
        cols = slice(c * FFN_COLS, (c + 1) * FFN_COLS)
        gate_buf[0:FFN_PAD, :] = ghist[c]
        gate_buf[FFN_PAD:, :] = _dot(h2_buf[...], w_up_ref[:, cols])
        ghist[c] = gate_buf[tile:, :]
        val_buf[...] = _dot(h2_buf[...], w_up_ref[:, D_FF + c * FFN_COLS:D_FF + (c + 1) * FFN_COLS])

        @pl.when(t == last)
        def _():
            ffnn_ref[0, :, cols] = gate_buf[FFN_PAD + tile - (FFN_CONV_WIDTH - 1):, :]

        def ffn_body(r, carry):
            r0 = pl.multiple_of(r * FFN_ROWS, FFN_ROWS)
            _ffn_block(gate_buf, r0, val_buf, act_buf, r0, cols, fcw_ref, fcb_ref)
            return carry

        lax.fori_loop(0, tile // FFN_ROWS, ffn_body, 0)
        x1_buf[...] += _dot(act_buf[...], w_down_ref[cols, :])

    y_ref[0] = _rms(x1_buf[...], gfin_ref[...])


def _sample_kernel(x_ref, cconv_ref, ck_ref, cv_ref, cffn_ref,
                   ga_ref, w_in_ref, cw_ref, cb_ref, lng_ref, lnb_ref, sinks_ref, goc_ref, goa_ref,
                   w_out_ref, gffn_ref, w_up_ref, fcw_ref, fcb_ref, w_down_ref, gfin_ref,
                   y_ref, convn_ref, kn_ref, vn_ref, ffnn_ref,
                   glu_buf, q_buf, kv_buf, kbd, vbd, ao_buf, mix_buf, x1_buf, h2_buf, gate_buf, val_buf, act_buf):
    streams, seq = glu_buf.shape[0], glu_buf.shape[1] - CONV_PAD
    n_keys = WINDOW + seq

    hb = _rms(x_ref[...], ga_ref[...]).astype(BF16)
    ab = _dot(hb, w_in_ref[:, 0:2 * D_CONV])
    glu = ab[:, 0:D_CONV] * jax.nn.sigmoid(ab[:, D_CONV:])
    glu_buf[:, 0:CONV_PAD, :] = cconv_ref[...]
    for s in range(streams):
        glu_buf[s, CONV_PAD:, :] = glu[s * seq:(s + 1) * seq, :]
    q_buf[...] = _dot(hb, w_in_ref[:, 2 * D_CONV:2 * D_CONV + D_ATTN]).astype(BF16)
    kv_buf[...] = _dot(hb, w_in_ref[:, 2 * D_CONV + D_ATTN:])
    for buf in (kbd, vbd):
        buf[:, :, n_keys:, :] = jnp.zeros((4, streams, KEY_PAD - n_keys, LANES), BF16)

    valid = lax.broadcasted_iota(jnp.int32, (seq, KEY_PAD), 1) < n_keys

    def stream_body(s, carry):
        row = pl.multiple_of(s * seq, seq)
        for cache_ref, new_ref, buf, lanes in ((ck_ref, kn_ref, kbd, slice(0, D_KV)),
                                               (cv_ref, vn_ref, vbd, slice(D_KV, 2 * D_KV))):
            hist = cache_ref[s]
            new = kv_buf[pl.ds(row, seq), lanes]
            for i, part in enumerate(_block_diag_parts(hist)):
                buf[i, s, 0:WINDOW, :] = part
            for i, part in enumerate(_block_diag_parts(new)):
                buf[i, s, WINDOW:n_keys, :] = part
            new_ref[s, 0:WINDOW - seq, :] = hist[seq:, :]
            new_ref[s, WINDOW - seq:, :] = new
        convn_ref[s] = glu_buf[s, CONV_PAD + seq - (CONV_WIDTH - 1):, :]
        n = _conv_mix_block(glu_buf.at[s], 0, seq, cw_ref, cb_ref, lng_ref, lnb_ref, goc_ref)
        mix_buf[pl.ds(row, seq), 0:D_CONV] = n.astype(BF16)
        _attn_block(q_buf, row, seq, lambda i: kbd[i, s], lambda i: vbd[i, s], valid, sinks_ref, ao_buf)
        return carry

    lax.fori_loop(0, streams, stream_body, 0)
    mix_buf[:, D_CONV:] = _rms(ao_buf[...], goa_ref[...]).astype(BF16)

    x1 = x_ref[...] + _dot(mix_buf[...], w_out_ref[...])
    x1_buf[...] = x1
    h2_buf[...] = _rms(x1, gffn_ref[...]).astype(BF16)

    blocks_per_stream = seq // FFN_ROWS
    for c in range(N_FFN_PASSES):
        cols = slice(c * FFN_COLS, (c + 1) * FFN_COLS)
        gate = _dot(h2_buf[...], w_up_ref[:, cols])
        gate_buf[:, 0:FFN_PAD, :] = cffn_ref[:, :, cols]
        for s in range(streams):
            gate_buf[s, FFN_PAD:, :] = gate[s * seq:(s + 1) * seq, :]
        val_buf[...] = _dot(h2_buf[...], w_up_ref[:, D_FF + c * FFN_COLS:D_FF + (c + 1) * FFN_COLS])
        ffnn_ref[:, :, cols] = gate_buf[:, FFN_PAD + seq - (FFN_CONV_WIDTH - 1):, :]

        def ffn_body(i, carry):
            s = i // blocks_per_stream
            r0 = pl.multiple_of((i % blocks_per_stream) * FFN_ROWS, FFN_ROWS)
            row = pl.multiple_of(i * FFN_ROWS, FFN_ROWS)
            _ffn_block(gate_buf.at[s], r0, val_buf, act_buf, row, cols, fcw_ref, fcb_ref)
            return carry

        lax.fori_loop(0, streams * blocks_per_stream, ffn_body, 0)
        x1_buf[...] += _dot(act_buf[...], w_down_ref[cols, :])

    y_ref[...] = _rms(x1_buf[...], gfin_ref[...])


def _resident(shape):
    zeros = (0,) * len(shape)
    return pl.BlockSpec(shape, lambda *_: zeros, pipeline_mode=pl.Buffered(1))


def _weight_specs():
    row = lambda n: _resident((1, n))
    return [
        row(D_MODEL),
        _resident((D_MODEL, D_IN)),
        _resident((CONV_WIDTH, D_CONV)),
        row(D_CONV), row(D_CONV), row(D_CONV),
        pl.BlockSpec(memory_space=pltpu.SMEM),
        row(D_CONV), row(D_ATTN),
        _resident((D_MIX, D_MODEL)),
        row(D_MODEL),
        _resident((D_MODEL, 2 * D_FF)),
        _resident((FFN_CONV_WIDTH, D_FF)),
        row(D_FF),
        _resident((D_FF, D_MODEL)),
        row(D_MODEL),
    ]


def _prompt_call(x, weights):
    batch, seq, _ = x.shape
    tile = PROMPT_TILE
    per_batch = lambda shape: pl.BlockSpec((1,) + shape, lambda b, t: (b, 0, 0))
    return pl.pallas_call(
        _prompt_kernel,
        grid=(batch, seq // tile),
        in_specs=[pl.BlockSpec((1, tile, D_MODEL), lambda b, t: (b, t, 0))] + _weight_specs(),
        out_specs=[
            pl.BlockSpec((1, tile, D_MODEL), lambda b, t: (b, t, 0)),
            per_batch((CONV_WIDTH - 1, D_CONV)),
            per_batch((WINDOW, D_KV)),
            per_batch((WINDOW, D_KV)),
            per_batch((FFN_CONV_WIDTH - 1, D_FF)),
        ],
        out_shape=[
            jax.ShapeDtypeStruct((batch, seq, D_MODEL), F32),
            jax.ShapeDtypeStruct((batch, CONV_WIDTH - 1, D_CONV), F32),
            jax.ShapeDtypeStruct((batch, WINDOW, D_KV), F32),
            jax.ShapeDtypeStruct((batch, WINDOW, D_KV), F32),
            jax.ShapeDtypeStruct((batch, FFN_CONV_WIDTH - 1, D_FF), F32),
        ],
        scratch_shapes=[
            pltpu.VMEM((CONV_PAD + tile, D_CONV), F32),
            pltpu.VMEM((tile, D_ATTN), BF16),
            pltpu.VMEM((4, WINDOW + tile + CHUNK, LANES), BF16),
            pltpu.VMEM((4, WINDOW + tile + CHUNK, LANES), BF16),
            pltpu.VMEM((tile, D_ATTN), F32),
            pltpu.VMEM((tile, D_MIX), BF16),
            pltpu.VMEM((tile, D_MODEL), F32),
            pltpu.VMEM((tile, D_MODEL), BF16),
            pltpu.VMEM((FFN_PAD + tile, FFN_COLS), F32),
            pltpu.VMEM((N_FFN_PASSES, FFN_PAD, FFN_COLS), F32),
            pltpu.VMEM((tile, FFN_COLS), F32),
            pltpu.VMEM((tile, FFN_COLS), BF16),
        ],
        compiler_params=pltpu.CompilerParams(
            dimension_semantics=("arbitrary", "arbitrary"),
            vmem_limit_bytes=VMEM_LIMIT_BYTES),
        name="prompt_layer",
    )(x, *weights)


def _sample_call(x, cconv, ck, cv, cffn, weights):
    n_streams, seq, _ = x.shape
    sb = SAMPLE_STREAMS
    rows = sb * seq
    per_stream = lambda shape: pl.BlockSpec((sb,) + shape, lambda i: (i, 0, 0))
    return pl.pallas_call(
        _sample_kernel,
        grid=(n_streams // sb,),
        in_specs=[
            pl.BlockSpec((rows, D_MODEL), lambda i: (i, 0)),
            per_stream((CONV_PAD, D_CONV)),
            per_stream((WINDOW, D_KV)),
            per_stream((WINDOW, D_KV)),
            per_stream((FFN_PAD, D_FF)),
        ] + _weight_specs(),
        out_specs=[
            pl.BlockSpec((rows, D_MODEL), lambda i: (i, 0)),
            per_stream((CONV_WIDTH - 1, D_CONV)),
            per_stream((WINDOW, D_KV)),
            per_stream((WINDOW, D_KV)),
            per_stream((FFN_CONV_WIDTH - 1, D_FF)),
        ],
        out_shape=[
            jax.ShapeDtypeStruct((n_streams * seq, D_MODEL), F32),
            jax.ShapeDtypeStruct((n_streams, CONV_WIDTH - 1, D_CONV), F32),
            jax.ShapeDtypeStruct((n_streams, WINDOW, D_KV), F32),
            jax.ShapeDtypeStruct((n_streams, WINDOW, D_KV), F32),
            jax.ShapeDtypeStruct((n_streams, FFN_CONV_WIDTH - 1, D_FF), F32),
        ],
        scratch_shapes=[
            pltpu.VMEM((sb, CONV_PAD + seq, D_CONV), F32),
            pltpu.VMEM((rows, D_ATTN), BF16),
            pltpu.VMEM((rows, 2 * D_KV), F32),
            pltpu.VMEM((4, sb, KEY_PAD, LANES), BF16),
            pltpu.VMEM((4, sb, KEY_PAD, LANES), BF16),
            pltpu.VMEM((rows, D_ATTN), F32),
            pltpu.VMEM((rows, D_MIX), BF16),
            pltpu.VMEM((rows, D_MODEL), F32),
            pltpu.VMEM((rows, D_MODEL), BF16),
            pltpu.VMEM((sb, FFN_PAD + seq, FFN_COLS), F32),
            pltpu.VMEM((rows, FFN_COLS), F32),
            pltpu.VMEM((rows, FFN_COLS), BF16),
        ],
        compiler_params=pltpu.CompilerParams(
            dimension_semantics=("arbitrary",),
            vmem_limit_bytes=VMEM_LIMIT_BYTES),
        name="sample_layer",
    )(x.reshape(n_streams * seq, D_MODEL), cconv, ck, cv, cffn, *weights)


def kernel(x_prompt, x_sample, cache_conv, cache_k, cache_v, cache_ffn_conv, g_attn_norm, w_in, conv_w, conv_b,
           conv_ln_g, conv_ln_b, sinks, g_out_conv, g_out_attn, w_out, g_ffn_norm, w_up, ffn_conv_w, ffn_conv_b,
           w_down, g_final):
    depth = w_in.shape[0]
    assert depth == 1, "single trunk layer"
    batch, seq, _ = x_prompt.shape
    n_streams, dec_seq, _ = x_sample.shape
    assert seq % PROMPT_TILE == 0 and PROMPT_TILE % CHUNK == 0 and PROMPT_TILE >= WINDOW
    assert n_streams % SAMPLE_STREAMS == 0 and dec_seq % FFN_ROWS == 0
    assert CONV_WIDTH - 1 <= dec_seq <= KEY_PAD - WINDOW

    row = lambda a: a.reshape(1, -1)
    weights = (
        row(g_attn_norm[0]), w_in[0].astype(BF16), conv_w[0], row(conv_b[0]), row(conv_ln_g[0]),
        row(conv_ln_b[0]), sinks[0], row(g_out_conv[0]), row(g_out_attn[0]), w_out[0].astype(BF16),
        row(g_ffn_norm[0]), w_up[0].astype(BF16), ffn_conv_w[0], row(ffn_conv_b[0]), w_down[0].astype(BF16),
        row(g_final),
    )

    y_p, conv_p, k_p, v_p, ffn_p = _prompt_call(x_prompt, weights)

    cconv = jnp.pad(cache_conv[0], ((0, 0), (CONV_PAD - (CONV_WIDTH - 1), 0), (0, 0)))
    cffn = jnp.pad(cache_ffn_conv[0], ((0, 0), (FFN_PAD - (FFN_CONV_WIDTH - 1), 0), (0, 0)))
    ck = cache_k[0].reshape(n_streams, WINDOW, D_KV)
    cv = cache_v[0].reshape(n_streams, WINDOW, D_KV)
    y_s, conv_s, k_s, v_s, ffn_s = _sample_call(x_sample, cconv, ck, cv, cffn, weights)

    heads = lambda a: a.reshape(depth, a.shape[0], WINDOW, N_KV_HEADS, HEAD_DIM)
    return (y_p, y_s.reshape(n_streams, dec_seq, D_MODEL),
            conv_p[None], heads(k_p), heads(v_p), ffn_p[None],
            conv_s[None], heads(k_s), heads(v_s), ffn_s[None])
```

```python
import math

import jax
import jax.numpy as jnp
from jax import lax
from jax.experimental import pallas as pl
from jax.experimental.pallas import tpu as pltpu

D_MODEL = 1024
CHUNK = 64
D_CONV = 512
CONV_WIDTH = 31
N_Q_HEADS = 8
N_KV_HEADS = 2
HEAD_DIM = 64
D_ATTN = N_Q_HEADS * HEAD_DIM
D_KV = N_KV_HEADS * HEAD_DIM
WINDOW = 128
D_MIX = D_CONV + D_ATTN
D_IN = 2 * D_CONV + D_ATTN + 2 * D_KV
D_FF = 2816
FFN_CONV_WIDTH = 3
EPS = 1e-6
NEG_INF = -1e30
SCALE = HEAD_DIM ** -0.5

SUBLANES = 8
LANES = 128
MXU_COLS = 256
KEY_BLOCK = MXU_COLS
QUERY_BLOCK = KEY_BLOCK - WINDOW
CONV_PAD = 32
CONV_ROWS = 64
FFN_PAD = SUBLANES
FFN_COLS = MXU_COLS
N_FFN_PASSES = D_FF // FFN_COLS
PROMPT_TILE = 512
SAMPLE_STREAMS = 8
VMEM_LIMIT_BYTES = 58 * 1024 * 1024

BF16 = jnp.bfloat16
F32 = jnp.float32


def _dot(a, b):
    return jnp.dot(a, b, preferred_element_type=F32)


def _dot_t(a, b):
    return lax.dot_general(a, b, (((1,), (1,)), ((), ())), preferred_element_type=F32)


def _rms(x, g):
    return x * lax.rsqrt(jnp.mean(x * x, axis=-1, keepdims=True) + EPS) * g


def _gelu_tanh(x):
    c = math.sqrt(2.0 / math.pi)
    half = 0.5 * x
    return half + half * jnp.tanh(x * (c + (c * 0.044715) * (x * x)))


def _block_diag_parts(a):
    lo = lax.broadcasted_iota(jnp.int32, a.shape, 1) < HEAD_DIM
    swapped = pltpu.roll(a, HEAD_DIM, 1)
    zero = jnp.zeros_like(a)
    parts = (jnp.where(lo, a, zero), jnp.where(lo, zero, swapped),
             jnp.where(lo, swapped, zero), jnp.where(lo, zero, a))
    return [p.astype(BF16) for p in parts]


def _conv_mix_block(glu_view, r0, rows, shift_buf, cw_ref, cb_ref, lng_ref, lnb_ref, goc_ref):
    first = CONV_PAD - (CONV_WIDTH - 1)
    span = rows + CONV_PAD - SUBLANES
    for a in range(1, SUBLANES):
        shift_buf[a - 1, 0:span, :] = glu_view[r0 + a:r0 + a + span, :]
    acc = jnp.broadcast_to(cb_ref[...], (rows, D_CONV))
    for j in range(CONV_WIDTH):
        a, base = (first + j) % SUBLANES, (first + j) // SUBLANES * SUBLANES
        if a == 0:
            src = glu_view[r0 + base:r0 + base + rows, :]
        else:
            src = shift_buf[a - 1, base:base + rows, :]
        acc = acc + cw_ref[j:j + 1, :] * src
    mu = jnp.mean(acc, axis=-1, keepdims=True)
    xc = acc - mu
    y = xc * lax.rsqrt(jnp.mean(xc * xc, axis=-1, keepdims=True) + EPS) * lng_ref[...] + lnb_ref[...]
    cy = y * jax.nn.sigmoid(y)
    return _rms(cy, goc_ref[...])


def _attn_block(q_ref, qrow, rows, kget, vget, valid, sinks_ref, out_ref):
    for kv_head in range(N_KV_HEADS):
        pairs = (2 * kv_head, 2 * kv_head + 1)
        q = jnp.concatenate([q_ref[qrow:qrow + rows, LANES * p:LANES * (p + 1)] for p in pairs], axis=0)
        pv, inv = [], []
        for half in range(2):
            s = _dot_t(q, kget(2 * kv_head + half))
            e_parts, inv_half = [], []
            for i, p in enumerate(pairs):
                sp = jnp.where(valid, s[i * rows:(i + 1) * rows, :], NEG_INF)
                sink = sinks_ref[2 * p + half]
                m = jnp.maximum(jnp.max(sp, axis=-1, keepdims=True), sink)
                e = jnp.exp(sp - m)
                inv_half.append(1.0 / (jnp.sum(e, axis=-1, keepdims=True) + jnp.exp(sink - m)))
                e_parts.append(e.astype(BF16))
            pv.append(_dot(jnp.concatenate(e_parts, axis=0), vget(2 * kv_head + half)))
            inv.append(inv_half)
        for i, p in enumerate(pairs):
            rs = slice(i * rows, (i + 1) * rows)
            out_ref[qrow:qrow + rows, LANES * p:LANES * (p + 1)] = pv[0][rs] * inv[0][i] + pv[1][rs] * inv[1][i]


def _ffn_pass(c, h2, gate_buf, w_up_ref, fcw_ref, fcb_ref, w_down_ref):
    segments, seg_rows = gate_buf.shape[0], gate_buf.shape[1] - FFN_PAD
    cols = slice(c * FFN_COLS, (c + 1) * FFN_COLS)
    gate = _dot(h2, w_up_ref[:, cols]).reshape(segments, seg_rows, FFN_COLS)
    gate_buf[:, FFN_PAD:, cols] = gate
    val = _dot(h2, w_up_ref[:, D_FF + c * FFN_COLS:D_FF + (c + 1) * FFN_COLS])
    gc = fcb_ref[:, cols] + fcw_ref[FFN_CONV_WIDTH - 1:FFN_CONV_WIDTH, cols] * gate
    for j in range(FFN_CONV_WIDTH - 1):
        back = FFN_CONV_WIDTH - 1 - j
        gc = gc + fcw_ref[j:j + 1, cols] * gate_buf[:, FFN_PAD - back:FFN_PAD - back + seg_rows, cols]
    act = _gelu_tanh(gc).reshape(segments * seg_rows, FFN_COLS) * val
    return _dot(act.astype(BF16), w_down_ref[cols, :])


def _prompt_kernel(x_ref, ga_ref, w_in_ref, cw_ref, cb_ref, lng_ref, lnb_ref, sinks_ref, goc_ref, goa_ref,
                   w_out_ref, gffn_ref, w_up_ref, fcw_ref, fcb_ref, w_down_ref, gfin_ref,
                   y_ref, convn_ref, kn_ref, vn_ref, ffnn_ref,
                   glu_buf, shift_buf, q_buf, kbd, vbd, ao_buf, mix_buf, x1_buf, gate_buf):
    t = pl.program_id(1)
    last = pl.num_programs(1) - 1
    tile = x_ref.shape[1]

    @pl.when(t == 0)
    def _():
        glu_buf[0:CONV_PAD, :] = jnp.zeros((CONV_PAD, D_CONV), F32)
        gate_buf[:, 0:FFN_PAD, :] = jnp.zeros((1, FFN_PAD, D_FF), F32)
        for buf in (kbd, vbd):
            buf[:, 0:WINDOW, :] = jnp.zeros((4, WINDOW, LANES), BF16)

    hb = _rms(x_ref[0], ga_ref[...]).astype(BF16)
    ab = _dot(hb, w_in_ref[:, 0:2 * D_CONV])
    glu_buf[CONV_PAD:, :] = ab[:, 0:D_CONV] * jax.nn.sigmoid(ab[:, D_CONV:])
    q_buf[...] = (_dot(hb, w_in_ref[:, 2 * D_CONV:2 * D_CONV + D_ATTN]) * SCALE).astype(BF16)
    kv = _dot(hb, w_in_ref[:, 2 * D_CONV + D_ATTN:])
    k = kv[:, 0:D_KV]
    v = kv[:, D_KV:]
    for i, part in enumerate(_block_diag_parts(k)):
        kbd[i, WINDOW:, :] = part
    for i, part in enumerate(_block_diag_parts(v)):
        vbd[i, WINDOW:, :] = part

    @pl.when(t == last)
    def _():
        kn_ref[0] = k[tile - WINDOW:, :]
        vn_ref[0] = v[tile - WINDOW:, :]
        convn_ref[0] = glu_buf[CONV_PAD + tile - (CONV_WIDTH - 1):, :]

    for r in range(tile // CONV_ROWS):
        r0 = r * CONV_ROWS
        n = _conv_mix_block(glu_buf, r0, CONV_ROWS, shift_buf.at[r % 2], cw_ref, cb_ref, lng_ref, lnb_ref, goc_ref)
        mix_buf[r0:r0 + CONV_ROWS, 0:D_CONV] = n.astype(BF16)
    glu_buf[0:CONV_PAD, :] = glu_buf[tile:, :]

    chunk_shift = CHUNK.bit_length() - 1
    q_chunk = lax.broadcasted_iota(jnp.int32, (QUERY_BLOCK, KEY_BLOCK), 0) >> chunk_shift
    k_chunk = lax.broadcasted_iota(jnp.int32, (QUERY_BLOCK, KEY_BLOCK), 1) >> chunk_shift
    band = (k_chunk >= q_chunk) & (k_chunk <= q_chunk + WINDOW // CHUNK)
    for b in range(tile // QUERY_BLOCK):
        r0 = b * QUERY_BLOCK
        valid = band & (k_chunk >= jnp.where(t == 0, WINDOW // CHUNK, 0)) if b == 0 else band
        _attn_block(q_buf, r0, QUERY_BLOCK,
                    lambda i: kbd[i, r0:r0 + KEY_BLOCK, :],
                    lambda i: vbd[i, r0:r0 + KEY_BLOCK, :],
                    valid, sinks_ref, ao_buf)
    for buf in (kbd, vbd):
        buf[:, 0:WINDOW, :] = buf[:, tile:, :]
    mix_buf[:, D_CONV:] = _rms(ao_buf[...], goa_ref[...]).astype(BF16)

    x1 = x_ref[0] + _dot(mix_buf[...], w_out_ref[...])
    x1_buf[...] = x1
    h2 = _rms(x1, gffn_ref[...]).astype(BF16)

    for c in range(N_FFN_PASSES):
        x1_buf[...] += _ffn_pass(c, h2, gate_buf, w_up_ref, fcw_ref, fcb_ref, w_down_ref)

    @pl.when(t == last)
    def _():
        ffnn_ref[...] = gate_buf[:, FFN_PAD + tile - (FFN_CONV_WIDTH - 1):, :]

    gate_buf[:, 0:FFN_PAD, :] = gate_buf[:, tile:, :]
    y_ref[0] = _rms(x1_buf[...], gfin_ref[...])


def _sample_kernel(x_ref, cconv_ref, ck_ref, cv_ref, cffn_ref,
                   ga_ref, w_in_ref, cw_ref, cb_ref, lng_ref, lnb_ref, sinks_ref, goc_ref, goa_ref,
                   w_out_ref, gffn_ref, w_up_ref, fcw_ref, fcb_ref, w_down_ref, gfin_ref,
                   y_ref, convn_ref, kn_ref, vn_ref, ffnn_ref,
                   glu_buf, shift_buf, q_buf, kbd, vbd, ao_buf, mix_buf, x1_buf, gate_buf):
    streams, seq = glu_buf.shape[0], glu_buf.shape[1] - CONV_PAD
    n_keys = WINDOW + seq

    hb = _rms(x_ref[...], ga_ref[...]).astype(BF16)
    ab = _dot(hb, w_in_ref[:, 0:2 * D_CONV])
    glu = ab[:, 0:D_CONV] * jax.nn.sigmoid(ab[:, D_CONV:])
    glu_buf[:, 0:CONV_PAD, :] = cconv_ref[...]
    glu_buf[:, CONV_PAD:, :] = glu.reshape(streams, seq, D_CONV)
    convn_ref[...] = glu_buf[:, CONV_PAD + seq - (CONV_WIDTH - 1):, :]
    q_buf[...] = (_dot(hb, w_in_ref[:, 2 * D_CONV:2 * D_CONV + D_ATTN]) * SCALE).astype(BF16)
    kv = _dot(hb, w_in_ref[:, 2 * D_CONV + D_ATTN:])
    for cache_ref, new_ref, buf, lanes in ((ck_ref, kn_ref, kbd, slice(0, D_KV)),
                                           (cv_ref, vn_ref, vbd, slice(D_KV, 2 * D_KV))):
        hist = cache_ref[...]
        new = kv[:, lanes].reshape(streams, seq, D_KV)
        new_ref[:, 0:WINDOW - seq, :] = hist[:, seq:, :]
        new_ref[:, WINDOW - seq:, :] = new
        for i, part in enumerate(_block_diag_parts(hist.reshape(streams * WINDOW, D_KV))):
            buf[i, :, 0:WINDOW, :] = part.reshape(streams, WINDOW, LANES)
        for i, part in enumerate(_block_diag_parts(kv[:, lanes])):
            buf[i, :, WINDOW:n_keys, :] = part.reshape(streams, seq, LANES)
        buf[:, :, n_keys:, :] = jnp.zeros((4, streams, KEY_BLOCK - n_keys, LANES), BF16)

    valid = lax.broadcasted_iota(jnp.int32, (seq, KEY_BLOCK), 1) < n_keys

    for s in range(streams):
        row = s * seq
        n = _conv_mix_block(glu_buf.at[s], 0, seq, shift_buf.at[s % 2], cw_ref, cb_ref, lng_ref, lnb_ref, goc_ref)
        mix_buf[row:row + seq, 0:D_CONV] = n.astype(BF16)
        _attn_block(q_buf, row, seq, lambda i: kbd[i, s], lambda i: vbd[i, s], valid, sinks_ref, ao_buf)
    mix_buf[:, D_CONV:] = _rms(ao_buf[...], goa_ref[...]).astype(BF16)

    x1 = x_ref[...] + _dot(mix_buf[...], w_out_ref[...])
    x1_buf[...] = x1
    h2 = _rms(x1, gffn_ref[...]).astype(BF16)

    gate_buf[:, 0:FFN_PAD, :] = cffn_ref[...]
    for c in range(N_FFN_PASSES):
        x1_buf[...] += _ffn_pass(c, h2, gate_buf, w_up_ref, fcw_ref, fcb_ref, w_down_ref)
    ffnn_ref[...] = gate_buf[:, FFN_PAD + seq - (FFN_CONV_WIDTH - 1):, :]
    y_ref[...] = _rms(x1_buf[...], gfin_ref[...])


def _resident(shape):
    zeros = (0,) * len(shape)
    return pl.BlockSpec(shape, lambda *_: zeros, pipeline_mode=pl.Buffered(1))


def _weight_specs():
    row = lambda n: _resident((1, n))
    return [
        row(D_MODEL),
        _resident((D_MODEL, D_IN)),
        _resident((CONV_WIDTH, D_CONV)),
        row(D_CONV), row(D_CONV), row(D_CONV),
        pl.BlockSpec(memory_space=pltpu.SMEM),
        row(D_CONV), row(D_ATTN),
        _resident((D_MIX, D_MODEL)),
        row(D_MODEL),
        _resident((D_MODEL, 2 * D_FF)),
        _resident((FFN_CONV_WIDTH, D_FF)),
        row(D_FF),
        _resident((D_FF, D_MODEL)),
        row(D_MODEL),
    ]


def _shift_scratch(rows):
    return pltpu.VMEM((2, SUBLANES - 1, rows + CONV_PAD - SUBLANES, D_CONV), F32)


def _prompt_call(x, weights):
    batch, seq, _ = x.shape
    tile = PROMPT_TILE
    per_batch = lambda shape: pl.BlockSpec((1,) + shape, lambda b, t: (b, 0, 0))
    return pl.pallas_call(
        _prompt_kernel,
        grid=(batch, seq // tile),
        in_specs=[pl.BlockSpec((1, tile, D_MODEL), lambda b, t: (b, t, 0))] + _weight_specs(),
        out_specs=[
            pl.BlockSpec((1, tile, D_MODEL), lambda b, t: (b, t, 0)),
            per_batch((CONV_WIDTH - 1, D_CONV)),
            per_batch((WINDOW, D_KV)),
            per_batch((WINDOW, D_KV)),
            per_batch((FFN_CONV_WIDTH - 1, D_FF)),
        ],
        out_shape=[
            jax.ShapeDtypeStruct((batch, seq, D_MODEL), F32),
            jax.ShapeDtypeStruct((batch, CONV_WIDTH - 1, D_CONV), F32),
            jax.ShapeDtypeStruct((batch, WINDOW, D_KV), F32),
            jax.ShapeDtypeStruct((batch, WINDOW, D_KV), F32),
            jax.ShapeDtypeStruct((batch, FFN_CONV_WIDTH - 1, D_FF), F32),
        ],
        scratch_shapes=[
            pltpu.VMEM((CONV_PAD + tile, D_CONV), F32),
            _shift_scratch(CONV_ROWS),
            pltpu.VMEM((tile, D_ATTN), BF16),
            pltpu.VMEM((4, WINDOW + tile, LANES), BF16),
            pltpu.VMEM((4, WINDOW + tile, LANES), BF16),
            pltpu.VMEM((tile, D_ATTN), F32),
            pltpu.VMEM((tile, D_MIX), BF16),
            pltpu.VMEM((tile, D_MODEL), F32),
            pltpu.VMEM((1, FFN_PAD + tile, D_FF), F32),
        ],
        compiler_params=pltpu.CompilerParams(
            dimension_semantics=("arbitrary", "arbitrary"),
            vmem_limit_bytes=VMEM_LIMIT_BYTES),
        name="prompt_layer",
    )(x, *weights)


def _sample_call(x, cconv, ck, cv, cffn, weights):
    n_streams, seq, _ = x.shape
    sb = SAMPLE_STREAMS
    rows = sb * seq
    per_stream = lambda shape: pl.BlockSpec((sb,) + shape, lambda i: (i, 0, 0))
    return pl.pallas_call(
        _sample_kernel,
        grid=(n_streams // sb,),
        in_specs=[
            pl.BlockSpec((rows, D_MODEL), lambda i: (i, 0)),
            per_stream((CONV_PAD, D_CONV)),
            per_stream((WINDOW, D_KV)),
            per_stream((WINDOW, D_KV)),
            per_stream((FFN_PAD, D_FF)),
        ] + _weight_specs(),
        out_specs=[
            pl.BlockSpec((rows, D_MODEL), lambda i: (i, 0)),
            per_stream((CONV_WIDTH - 1, D_CONV)),
            per_stream((WINDOW, D_KV)),
            per_stream((WINDOW, D_KV)),
            per_stream((FFN_CONV_WIDTH - 1, D_FF)),
        ],
        out_shape=[
            jax.ShapeDtypeStruct((n_streams * seq, D_MODEL), F32),
            jax.ShapeDtypeStruct((n_streams, CONV_WIDTH - 1, D_CONV), F32),
            jax.ShapeDtypeStruct((n_streams, WINDOW, D_KV), F32),
            jax.ShapeDtypeStruct((n_streams, WINDOW, D_KV), F32),
            jax.ShapeDtypeStruct((n_streams, FFN_CONV_WIDTH - 1, D_FF), F32),
        ],
        scratch_shapes=[
            pltpu.VMEM((sb, CONV_PAD + seq, D_CONV), F32),
            _shift_scratch(seq),
            pltpu.VMEM((rows, D_ATTN), BF16),
            pltpu.VMEM((4, sb, KEY_BLOCK, LANES), BF16),
            pltpu.VMEM((4, sb, KEY_BLOCK, LANES), BF16),
            pltpu.VMEM((rows, D_ATTN), F32),
            pltpu.VMEM((rows, D_MIX), BF16),
            pltpu.VMEM((rows, D_MODEL), F32),
            pltpu.VMEM((sb, FFN_PAD + seq, D_FF), F32),
        ],
        compiler_params=pltpu.CompilerParams(
            dimension_semantics=("arbitrary",),
            vmem_limit_bytes=VMEM_LIMIT_BYTES),
        name="sample_layer",
    )(x.reshape(n_streams * seq, D_MODEL), cconv, ck, cv, cffn, *weights)


def kernel(x_prompt, x_sample, cache_conv, cache_k, cache_v, cache_ffn_conv, g_attn_norm, w_in, conv_w, conv_b,
           conv_ln_g, conv_ln_b, sinks, g_out_conv, g_out_attn, w_out, g_ffn_norm, w_up, ffn_conv_w, ffn_conv_b,
           w_down, g_final):
    depth = w_in.shape[0]
    assert depth == 1, "single trunk layer"
    batch, seq, _ = x_prompt.shape
    n_streams, dec_seq, _ = x_sample.shape
    assert seq % PROMPT_TILE == 0 and PROMPT_TILE % QUERY_BLOCK == 0 and PROMPT_TILE % CONV_ROWS == 0
    assert n_streams % SAMPLE_STREAMS == 0 and dec_seq % (2 * SUBLANES) == 0
    assert CONV_WIDTH - 1 <= dec_seq <= KEY_BLOCK - WINDOW

    row = lambda a: a.reshape(1, -1)
    weights = (
        row(g_attn_norm[0]), w_in[0].astype(BF16), conv_w[0], row(conv_b[0]), row(conv_ln_g[0]),
        row(conv_ln_b[0]), sinks[0], row(g_out_conv[0]), row(g_out_attn[0]), w_out[0].astype(BF16),
        row(g_ffn_norm[0]), w_up[0].astype(BF16), ffn_conv_w[0], row(ffn_conv_b[0]), w_down[0].astype(BF16),
        row(g_final),
    )

    y_p, conv_p, k_p, v_p, ffn_p = _prompt_call(x_prompt, weights)

    cconv = jnp.pad(cache_conv[0], ((0, 0), (CONV_PAD - (CONV_WIDTH - 1), 0), (0, 0)))
    cffn = jnp.pad(cache_ffn_conv[0], ((0, 0), (FFN_PAD - (FFN_CONV_WIDTH - 1), 0), (0, 0)))
    ck = cache_k[0].reshape(n_streams, WINDOW, D_KV)
    cv = cache_v[0].reshape(n_streams, WINDOW, D_KV)
    y_s, conv_s, k_s, v_s, ffn_s = _sample_call(x_sample, cconv, ck, cv, cffn, weights)

    heads = lambda a: a.reshape(depth, a.shape[0], WINDOW, N_KV_HEADS, HEAD_DIM)
    return (y_p, y_s.reshape(n_streams, dec_seq, D_MODEL),
            conv_p[None], heads(k_p), heads(v_p), ffn_p[None],
            conv_s[None], heads(k_s), heads(v_s), ffn_s[None])
```

```python
import functools
import math

import jax
import jax.numpy as jnp
from jax import lax
from jax.experimental import pallas as pl
from jax.experimental.pallas import tpu as pltpu

D_MODEL = 1024
CHUNK = 64
D_CONV = 512
CONV_WIDTH = 31
N_Q_HEADS = 8
N_KV_HEADS = 2
HEAD_DIM = 64
D_ATTN = N_Q_HEADS * HEAD_DIM
D_KV = N_KV_HEADS * HEAD_DIM
WINDOW = 128
D_MIX = D_CONV + D_ATTN
D_IN = 2 * D_CONV + D_ATTN + 2 * D_KV
D_FF = 2816
FFN_CONV_WIDTH = 3
EPS = 1e-6
NEG_INF = -1e30
SCALE = HEAD_DIM ** -0.5

SUBLANES = 8
LANES = 128
MXU_COLS = 256
KEY_BLOCK = MXU_COLS
QUERY_BLOCK = KEY_BLOCK - WINDOW
CONV_PAD = 32
CONV_ROWS = 64
FFN_PAD = SUBLANES
FFN_COLS = MXU_COLS
N_FFN_PASSES = D_FF // FFN_COLS
PROMPT_TILE = 512
SAMPLE_STREAMS = 8
VMEM_LIMIT_BYTES = 58 * 1024 * 1024

BF16 = jnp.bfloat16
F32 = jnp.float32


def _dot(a, b):
    return jnp.dot(a, b, preferred_element_type=F32)


def _dot_t(a, b):
    return lax.dot_general(a, b, (((1,), (1,)), ((), ())), preferred_element_type=F32)


def _rms(x, g):
    return x * lax.rsqrt(jnp.mean(x * x, axis=-1, keepdims=True) + EPS) * g


def _gelu_tanh(x):
    c = math.sqrt(2.0 / math.pi)
    half = 0.5 * x
    return half + half * jnp.tanh(x * (c + (c * 0.044715) * (x * x)))


def _block_diag_parts(a):
    lo = lax.broadcasted_iota(jnp.int32, a.shape, 1) < HEAD_DIM
    swapped = pltpu.roll(a, HEAD_DIM, 1)
    zero = jnp.zeros_like(a)
    parts = (jnp.where(lo, a, zero), jnp.where(lo, zero, swapped),
             jnp.where(lo, swapped, zero), jnp.where(lo, zero, a))
    return [p.astype(BF16) for p in parts]


def _conv_mix_block(glu_view, r0, rows, shift_buf, cw_ref, cb_ref, lng_ref, lnb_ref, goc_ref):
    first = CONV_PAD - (CONV_WIDTH - 1)
    span = rows + CONV_PAD - SUBLANES
    for a in range(1, SUBLANES):
        shift_buf[a - 1, 0:span, :] = glu_view[r0 + a:r0 + a + span, :]
    acc = jnp.broadcast_to(cb_ref[...], (rows, D_CONV))
    for j in range(CONV_WIDTH):
        a, base = (first + j) % SUBLANES, (first + j) // SUBLANES * SUBLANES
        if a == 0:
            src = glu_view[r0 + base:r0 + base + rows, :]
        else:
            src = shift_buf[a - 1, base:base + rows, :]
        acc = acc + cw_ref[j:j + 1, :] * src
    mu = jnp.mean(acc, axis=-1, keepdims=True)
    xc = acc - mu
    y = xc * lax.rsqrt(jnp.mean(xc * xc, axis=-1, keepdims=True) + EPS) * lng_ref[...] + lnb_ref[...]
    cy = y * jax.nn.sigmoid(y)
    return _rms(cy, goc_ref[...])


def _attn_block(q_ref, qrow, rows, kget, vget, valid, sinks_ref, out_ref):
    for kv_head in range(N_KV_HEADS):
        pairs = (2 * kv_head, 2 * kv_head + 1)
        q = jnp.concatenate([q_ref[qrow:qrow + rows, LANES * p:LANES * (p + 1)] for p in pairs], axis=0)
        pv, inv = [], []
        for half in range(2):
            s = _dot_t(q, kget(2 * kv_head + half))
            e_parts, inv_half = [], []
            for i, p in enumerate(pairs):
                sp = jnp.where(valid, s[i * rows:(i + 1) * rows, :], NEG_INF)
                sink = sinks_ref[2 * p + half]
                m = jnp.maximum(jnp.max(sp, axis=-1, keepdims=True), sink)
                e = jnp.exp(sp - m)
                inv_half.append(1.0 / (jnp.sum(e, axis=-1, keepdims=True) + jnp.exp(sink - m)))
                e_parts.append(e.astype(BF16))
            pv.append(_dot(jnp.concatenate(e_parts, axis=0), vget(2 * kv_head + half)))
            inv.append(inv_half)
        for i, p in enumerate(pairs):
            rs = slice(i * rows, (i + 1) * rows)
            out_ref[qrow:qrow + rows, LANES * p:LANES * (p + 1)] = pv[0][rs] * inv[0][i] + pv[1][rs] * inv[1][i]


def _ffn_up(c, h2, gate_buf, val_buf, w_up_ref):
    segments, seg_rows = gate_buf.shape[0], gate_buf.shape[1] - FFN_PAD
    cols = slice(c * FFN_COLS, (c + 1) * FFN_COLS)
    gate_buf[:, FFN_PAD:, cols] = _dot(h2, w_up_ref[:, cols]).reshape(segments, seg_rows, FFN_COLS)
    val_buf[c % 2] = _dot(h2, w_up_ref[:, D_FF + c * FFN_COLS:D_FF + (c + 1) * FFN_COLS])


def _ffn_act(c, gate_buf, lag_buf, val_buf, act_buf, fcw_ref, fcb_ref):
    segments, seg_rows = gate_buf.shape[0], gate_buf.shape[1] - FFN_PAD
    cols = slice(c * FFN_COLS, (c + 1) * FFN_COLS)
    gc = fcb_ref[:, cols] + fcw_ref[FFN_CONV_WIDTH - 1:FFN_CONV_WIDTH, cols] * gate_buf[:, FFN_PAD:, cols]
    for lag in range(1, FFN_CONV_WIDTH):
        lag_buf[c % 2, lag - 1] = gate_buf[:, FFN_PAD - lag:FFN_PAD - lag + seg_rows, cols]
        j = FFN_CONV_WIDTH - 1 - lag
        gc = gc + fcw_ref[j:j + 1, cols] * lag_buf[c % 2, lag - 1]
    act = _gelu_tanh(gc).reshape(segments * seg_rows, FFN_COLS) * val_buf[c % 2]
    act_buf[c % 2] = act.astype(BF16)


def _ffn_down(c, act_buf, w_down_ref):
    return _dot(act_buf[c % 2], w_down_ref[c * FFN_COLS:(c + 1) * FFN_COLS, :])


def _prompt_kernel(x_ref, ga_ref, w_in_ref, cw_ref, cb_ref, lng_ref, lnb_ref, sinks_ref, goc_ref, goa_ref,
                   w_out_ref, gffn_ref, w_up_ref, fcw_ref, fcb_ref, w_down_ref, gfin_ref,
                   y_ref, convn_ref, kn_ref, vn_ref, ffnn_ref,
                   glu_buf, shift_buf, q_buf, kbd, vbd, ao_buf, mix_buf, x1_buf, gate_buf, lag_buf, val_buf, act_buf,
                   *, tiles_per_stream):
    i = pl.program_id(0)
    tile = x_ref.shape[1]
    mixer_starts_stream = i % tiles_per_stream == 0
    ffn_starts_stream = (i + tiles_per_stream - 1) % tiles_per_stream == 0

    @pl.when(i == 0)
    def _():
        x1_buf[...] = jnp.zeros(x1_buf.shape, F32)

    @pl.when(mixer_starts_stream)
    def _():
        glu_buf[0:CONV_PAD, :] = jnp.zeros((CONV_PAD, D_CONV), F32)
        for buf in (kbd, vbd):
            buf[:, 0:WINDOW, :] = jnp.zeros((4, WINDOW, LANES), BF16)

    @pl.when(jnp.logical_or(i == 0, ffn_starts_stream))
    def _():
        gate_buf[:, 0:FFN_PAD, :] = jnp.zeros((1, FFN_PAD, D_FF), F32)

    hb = _rms(x_ref[0], ga_ref[...]).astype(BF16)
    ab = _dot(hb, w_in_ref[:, 0:2 * D_CONV])
    glu_buf[CONV_PAD:, :] = ab[:, 0:D_CONV] * jax.nn.sigmoid(ab[:, D_CONV:])

    x1_prev = x1_buf[...]
    y_ref[0] = x1_prev
    h2 = _rms(x1_prev, gffn_ref[...]).astype(BF16)
    _ffn_up(0, h2, gate_buf, val_buf, w_up_ref)

    q_buf[...] = (_dot(hb, w_in_ref[:, 2 * D_CONV:2 * D_CONV + D_ATTN]) * SCALE).astype(BF16)
    kv = _dot(hb, w_in_ref[:, 2 * D_CONV + D_ATTN:])
    k = kv[:, 0:D_KV]
    v = kv[:, D_KV:]
    for i, part in enumerate(_block_diag_parts(k)):
        kbd[i, WINDOW:, :] = part
    for i, part in enumerate(_block_diag_parts(v)):
        vbd[i, WINDOW:, :] = part

    kn_ref[0] = k[tile - WINDOW:, :]
    vn_ref[0] = v[tile - WINDOW:, :]
    convn_ref[0] = glu_buf[CONV_PAD + tile - (CONV_WIDTH - 1):, :]

    def conv_task(r):
        r0 = r * CONV_ROWS
        n = _conv_mix_block(glu_buf, r0, CONV_ROWS, shift_buf.at[r % 2], cw_ref, cb_ref, lng_ref, lnb_ref, goc_ref)
        mix_buf[r0:r0 + CONV_ROWS, 0:D_CONV] = n.astype(BF16)

    chunk_shift = CHUNK.bit_length() - 1
    q_chunk = lax.broadcasted_iota(jnp.int32, (QUERY_BLOCK, KEY_BLOCK), 0) >> chunk_shift
    k_chunk = lax.broadcasted_iota(jnp.int32, (QUERY_BLOCK, KEY_BLOCK), 1) >> chunk_shift
    band = (k_chunk >= q_chunk) & (k_chunk <= q_chunk + WINDOW // CHUNK)

    def attn_task(b):
        r0 = b * QUERY_BLOCK
        valid = band & (k_chunk >= jnp.where(mixer_starts_stream, WINDOW // CHUNK, 0)) if b == 0 else band
        _attn_block(q_buf, r0, QUERY_BLOCK,
                    lambda i: kbd[i, r0:r0 + KEY_BLOCK, :],
                    lambda i: vbd[i, r0:r0 + KEY_BLOCK, :],
                    valid, sinks_ref, ao_buf)

    convs_per_attn = (tile // CONV_ROWS) // (tile // QUERY_BLOCK)
    mixer_tasks = []
    for b in range(tile // QUERY_BLOCK):
        mixer_tasks += [functools.partial(conv_task, b * convs_per_attn + r) for r in range(convs_per_attn)]
        mixer_tasks.append(functools.partial(attn_task, b))
    for c in range(N_FFN_PASSES):
        if c + 1 < N_FFN_PASSES:
            _ffn_up(c + 1, h2, gate_buf, val_buf, w_up_ref)
        _ffn_act(c, gate_buf, lag_buf, val_buf, act_buf, fcw_ref, fcb_ref)
        lo = len(mixer_tasks) * c // N_FFN_PASSES
        hi = len(mixer_tasks) * (c + 1) // N_FFN_PASSES
        for task in mixer_tasks[lo:hi]:
            task()
        y_ref[0] += _ffn_down(c, act_buf, w_down_ref)

    glu_buf[0:CONV_PAD, :] = glu_buf[tile:, :]
    for buf in (kbd, vbd):
        buf[:, 0:WINDOW, :] = buf[:, tile:, :]
    mix_buf[:, D_CONV:] = _rms(ao_buf[...], goa_ref[...]).astype(BF16)

    ffnn_ref[...] = gate_buf[:, FFN_PAD + tile - (FFN_CONV_WIDTH - 1):, :]
    gate_buf[:, 0:FFN_PAD, :] = gate_buf[:, tile:, :]
    y_ref[0] = _rms(y_ref[0], gfin_ref[...])

    x1_buf[...] = x_ref[0] + _dot(mix_buf[...], w_out_ref[...])


def _sample_kernel(x_ref, cconv_ref, ck_ref, cv_ref, cffn_ref,
                   ga_ref, w_in_ref, cw_ref, cb_ref, lng_ref, lnb_ref, sinks_ref, goc_ref, goa_ref,
                   w_out_ref, gffn_ref, w_up_ref, fcw_ref, fcb_ref, w_down_ref, gfin_ref,
                   y_ref, convn_ref, kn_ref, vn_ref, ffnn_ref,
                   glu_buf, shift_buf, q_buf, kbd, vbd, ao_buf, mix_buf, x1_buf, gate_buf, lag_buf, val_buf,
                   act_buf):
    streams, seq = glu_buf.shape[0], glu_buf.shape[1] - CONV_PAD
    n_keys = WINDOW + seq

    hb = _rms(x_ref[...], ga_ref[...]).astype(BF16)
    ab = _dot(hb, w_in_ref[:, 0:2 * D_CONV])
    glu = ab[:, 0:D_CONV] * jax.nn.sigmoid(ab[:, D_CONV:])
    glu_buf[:, 0:CONV_PAD, :] = cconv_ref[...]
    glu_buf[:, CONV_PAD:, :] = glu.reshape(streams, seq, D_CONV)
    convn_ref[...] = glu_buf[:, CONV_PAD + seq - (CONV_WIDTH - 1):, :]
    q_buf[...] = (_dot(hb, w_in_ref[:, 2 * D_CONV:2 * D_CONV + D_ATTN]) * SCALE).astype(BF16)
    kv = _dot(hb, w_in_ref[:, 2 * D_CONV + D_ATTN:])
    for cache_ref, new_ref, buf, lanes in ((ck_ref, kn_ref, kbd, slice(0, D_KV)),
                                           (cv_ref, vn_ref, vbd, slice(D_KV, 2 * D_KV))):
        hist = cache_ref[...]
        new = kv[:, lanes].reshape(streams, seq, D_KV)
        new_ref[:, 0:WINDOW - seq, :] = hist[:, seq:, :]
        new_ref[:, WINDOW - seq:, :] = new
        for i, part in enumerate(_block_diag_parts(hist.reshape(streams * WINDOW, D_KV))):
            buf[i, :, 0:WINDOW, :] = part.reshape(streams, WINDOW, LANES)
        for i, part in enumerate(_block_diag_parts(kv[:, lanes])):
            buf[i, :, WINDOW:n_keys, :] = part.reshape(streams, seq, LANES)
        buf[:, :, n_keys:, :] = jnp.zeros((4, streams, KEY_BLOCK - n_keys, LANES), BF16)

    valid = lax.broadcasted_iota(jnp.int32, (seq, KEY_BLOCK), 1) < n_keys

    for s in range(streams):
        row = s * seq
        n = _conv_mix_block(glu_buf.at[s], 0, seq, shift_buf.at[s % 2], cw_ref, cb_ref, lng_ref, lnb_ref, goc_ref)
        mix_buf[row:row + seq, 0:D_CONV] = n.astype(BF16)
        _attn_block(q_buf, row, seq, lambda i: kbd[i, s], lambda i: vbd[i, s], valid, sinks_ref, ao_buf)
    mix_buf[:, D_CONV:] = _rms(ao_buf[...], goa_ref[...]).astype(BF16)

    x1 = x_ref[...] + _dot(mix_buf[...], w_out_ref[...])
    x1_buf[...] = x1
    h2 = _rms(x1, gffn_ref[...]).astype(BF16)

    gate_buf[:, 0:FFN_PAD, :] = cffn_ref[...]
    _ffn_up(0, h2, gate_buf, val_buf, w_up_ref)
    for c in range(N_FFN_PASSES):
        if c + 1 < N_FFN_PASSES:
            _ffn_up(c + 1, h2, gate_buf, val_buf, w_up_ref)
        _ffn_act(c, gate_buf, lag_buf, val_buf, act_buf, fcw_ref, fcb_ref)
        x1_buf[...] += _ffn_down(c, act_buf, w_down_ref)
    ffnn_ref[...] = gate_buf[:, FFN_PAD + seq - (FFN_CONV_WIDTH - 1):, :]
    y_ref[...] = _rms(x1_buf[...], gfin_ref[...])


def _resident(shape):
    zeros = (0,) * len(shape)
    return pl.BlockSpec(shape, lambda *_: zeros, pipeline_mode=pl.Buffered(1))


def _weight_specs():
    row = lambda n: _resident((1, n))
    return [
        row(D_MODEL),
        _resident((D_MODEL, D_IN)),
        _resident((CONV_WIDTH, D_CONV)),
        row(D_CONV), row(D_CONV), row(D_CONV),
        pl.BlockSpec(memory_space=pltpu.SMEM),
        row(D_CONV), row(D_ATTN),
        _resident((D_MIX, D_MODEL)),
        row(D_MODEL),
        _resident((D_MODEL, 2 * D_FF)),
        _resident((FFN_CONV_WIDTH, D_FF)),
        row(D_FF),
        _resident((D_FF, D_MODEL)),
        row(D_MODEL),
    ]


def _shift_scratch(rows):
    return pltpu.VMEM((2, SUBLANES - 1, rows + CONV_PAD - SUBLANES, D_CONV), F32)


def _prompt_call(x, weights):
    batch, seq, _ = x.shape
    tile = PROMPT_TILE
    per_stream = seq // tile
    n_tiles = batch * per_stream
    mixer_tile = lambda i: jnp.minimum(i, n_tiles - 1)
    ffn_tile = lambda i: jnp.maximum(i - 1, 0)
    mixer_out = lambda shape: pl.BlockSpec((1,) + shape, lambda i: (mixer_tile(i) // per_stream, 0, 0))
    return pl.pallas_call(
        functools.partial(_prompt_kernel, tiles_per_stream=per_stream),
        grid=(n_tiles + 1,),
        in_specs=[pl.BlockSpec((1, tile, D_MODEL),
                               lambda i: (mixer_tile(i) // per_stream, mixer_tile(i) % per_stream, 0))]
        + _weight_specs(),
        out_specs=[
            pl.BlockSpec((1, tile, D_MODEL), lambda i: (ffn_tile(i) // per_stream, ffn_tile(i) % per_stream, 0)),
            mixer_out((CONV_WIDTH - 1, D_CONV)),
            mixer_out((WINDOW, D_KV)),
            mixer_out((WINDOW, D_KV)),
            pl.BlockSpec((1, FFN_CONV_WIDTH - 1, D_FF), lambda i: (ffn_tile(i) // per_stream, 0, 0)),
        ],
        out_shape=[
            jax.ShapeDtypeStruct((batch, seq, D_MODEL), F32),
            jax.ShapeDtypeStruct((batch, CONV_WIDTH - 1, D_CONV), F32),
            jax.ShapeDtypeStruct((batch, WINDOW, D_KV), F32),
            jax.ShapeDtypeStruct((batch, WINDOW, D_KV), F32),
            jax.ShapeDtypeStruct((batch, FFN_CONV_WIDTH - 1, D_FF), F32),
        ],
        scratch_shapes=[
            pltpu.VMEM((CONV_PAD + tile, D_CONV), F32),
            _shift_scratch(CONV_ROWS),
            pltpu.VMEM((tile, D_ATTN), BF16),
            pltpu.VMEM((4, WINDOW + tile, LANES), BF16),
            pltpu.VMEM((4, WINDOW + tile, LANES), BF16),
            pltpu.VMEM((tile, D_ATTN), F32),
            pltpu.VMEM((tile, D_MIX), BF16),
            pltpu.VMEM((tile, D_MODEL), F32),
            pltpu.VMEM((1, FFN_PAD + tile, D_FF), F32),
            pltpu.VMEM((2, FFN_CONV_WIDTH - 1, 1, tile, FFN_COLS), F32),
            pltpu.VMEM((2, tile, FFN_COLS), F32),
            pltpu.VMEM((2, tile, FFN_COLS), BF16),
        ],
        compiler_params=pltpu.CompilerParams(
            dimension_semantics=("arbitrary",),
            vmem_limit_bytes=VMEM_LIMIT_BYTES),
        name="prompt_layer",
    )(x, *weights)


def _sample_call(x, cconv, ck, cv, cffn, weights):
    n_streams, seq, _ = x.shape
    sb = SAMPLE_STREAMS
    rows = sb * seq
    per_stream = lambda shape: pl.BlockSpec((sb,) + shape, lambda i: (i, 0, 0))
    return pl.pallas_call(
        _sample_kernel,
        grid=(n_streams // sb,),
        in_specs=[
            pl.BlockSpec((rows, D_MODEL), lambda i: (i, 0)),
            per_stream((CONV_PAD, D_CONV)),
            per_stream((WINDOW, D_KV)),
            per_stream((WINDOW, D_KV)),
            per_stream((FFN_PAD, D_FF)),
        ] + _weight_specs(),
        out_specs=[
            pl.BlockSpec((rows, D_MODEL), lambda i: (i, 0)),
            per_stream((CONV_WIDTH - 1, D_CONV)),
            per_stream((WINDOW, D_KV)),
            per_stream((WINDOW, D_KV)),
            per_stream((FFN_CONV_WIDTH - 1, D_FF)),
        ],
        out_shape=[
            jax.ShapeDtypeStruct((n_streams * seq, D_MODEL), F32),
            jax.ShapeDtypeStruct((n_streams, CONV_WIDTH - 1, D_CONV), F32),
            jax.ShapeDtypeStruct((n_streams, WINDOW, D_KV), F32),
            jax.ShapeDtypeStruct((n_streams, WINDOW, D_KV), F32),
            jax.ShapeDtypeStruct((n_streams, FFN_CONV_WIDTH - 1, D_FF), F32),
        ],
        scratch_shapes=[
            pltpu.VMEM((sb, CONV_PAD + seq, D_CONV), F32),
            _shift_scratch(seq),
            pltpu.VMEM((rows, D_ATTN), BF16),
            pltpu.VMEM((4, sb, KEY_BLOCK, LANES), BF16),
            pltpu.VMEM((4, sb, KEY_BLOCK, LANES), BF16),
            pltpu.VMEM((rows, D_ATTN), F32),
            pltpu.VMEM((rows, D_MIX), BF16),
            pltpu.VMEM((rows, D_MODEL), F32),
            pltpu.VMEM((sb, FFN_PAD + seq, D_FF), F32),
            pltpu.VMEM((2, FFN_CONV_WIDTH - 1, sb, seq, FFN_COLS), F32),
            pltpu.VMEM((2, rows, FFN_COLS), F32),
            pltpu.VMEM((2, rows, FFN_COLS), BF16),
        ],
        compiler_params=pltpu.CompilerParams(
            dimension_semantics=("arbitrary",),
            vmem_limit_bytes=VMEM_LIMIT_BYTES),
        name="sample_layer",
    )(x.reshape(n_streams * seq, D_MODEL), cconv, ck, cv, cffn, *weights)


def kernel(x_prompt, x_sample, cache_conv, cache_k, cache_v, cache_ffn_conv, g_attn_norm, w_in, conv_w, conv_b,
           conv_ln_g, conv_ln_b, sinks, g_out_conv, g_out_attn, w_out, g_ffn_norm, w_up, ffn_conv_w, ffn_conv_b,
           w_down, g_final):
    depth = w_in.shape[0]
    assert depth == 1, "single trunk layer"
    batch, seq, _ = x_prompt.shape
    n_streams, dec_seq, _ = x_sample.shape
    assert seq % PROMPT_TILE == 0 and PROMPT_TILE % QUERY_BLOCK == 0 and PROMPT_TILE % CONV_ROWS == 0
    assert n_streams % SAMPLE_STREAMS == 0 and dec_seq % (2 * SUBLANES) == 0
    assert CONV_WIDTH - 1 <= dec_seq <= KEY_BLOCK - WINDOW

    row = lambda a: a.reshape(1, -1)
    weights = (
        row(g_attn_norm[0]), w_in[0].astype(BF16), conv_w[0], row(conv_b[0]), row(conv_ln_g[0]),
        row(conv_ln_b[0]), sinks[0], row(g_out_conv[0]), row(g_out_attn[0]), w_out[0].astype(BF16),
        row(g_ffn_norm[0]), w_up[0].astype(BF16), ffn_conv_w[0], row(ffn_conv_b[0]), w_down[0].astype(BF16),
        row(g_final),
    )

    y_p, conv_p, k_p, v_p, ffn_p = _prompt_call(x_prompt, weights)

    cconv = jnp.pad(cache_conv[0], ((0, 0), (CONV_PAD - (CONV_WIDTH - 1), 0), (0, 0)))
    cffn = jnp.pad(cache_ffn_conv[0], ((0, 0), (FFN_PAD - (FFN_CONV_WIDTH - 1), 0), (0, 0)))
    ck = cache_k[0].reshape(n_streams, WINDOW, D_KV)
    cv = cache_v[0].reshape(n_streams, WINDOW, D_KV)
    y_s, conv_s, k_s, v_s, ffn_s = _sample_call(x_sample, cconv, ck, cv, cffn, weights)

    heads = lambda a: a.reshape(depth, a.shape[0], WINDOW, N_KV_HEADS, HEAD_DIM)
    return (y_p, y_s.reshape(n_streams, dec_seq, D_MODEL),
            conv_p[None], heads(k_p), heads(v_p), ffn_p[None],
            conv_s[None], heads(k_s), heads(v_s), ffn_s[None])
```

```python
import functools
import math

import jax
import jax.numpy as jnp
from jax import lax
from jax.experimental import pallas as pl
from jax.experimental.pallas import tpu as pltpu

D_MODEL = 1024
CHUNK = 64
D_CONV = 512
CONV_WIDTH = 31
N_Q_HEADS = 8
N_KV_HEADS = 2
HEAD_DIM = 64
D_ATTN = N_Q_HEADS * HEAD_DIM
D_KV = N_KV_HEADS * HEAD_DIM
WINDOW = 128
D_MIX = D_CONV + D_ATTN
D_IN = 2 * D_CONV + D_ATTN + 2 * D_KV
D_FF = 2816
FFN_CONV_WIDTH = 3
EPS = 1e-6
NEG_INF = -1e30
SCALE = HEAD_DIM ** -0.5

SUBLANES = 8
LANES = 128
MXU_COLS = 256
KEY_BLOCK = MXU_COLS
QUERY_BLOCK = KEY_BLOCK - WINDOW
CONV_PAD = 32
CONV_ROWS = 64
FFN_PAD = SUBLANES
FFN_COLS = MXU_COLS
N_FFN_PASSES = D_FF // FFN_COLS
PROMPT_TILE = 512
MIXER_ORDER = (0, 1, 2, 8, 3, 4, 5, 9, 6, 7, 10, 11)
CONV_PROJECTION_PASS = 9
SAMPLE_STREAMS = 8
VMEM_LIMIT_BYTES = 58 * 1024 * 1024

BF16 = jnp.bfloat16
F32 = jnp.float32


def _dot(a, b):
    return jnp.dot(a, b, preferred_element_type=F32)


def _dot_t(a, b):
    return lax.dot_general(a, b, (((1,), (1,)), ((), ())), preferred_element_type=F32)


def _rms(x, g):
    return x * lax.rsqrt(jnp.mean(x * x, axis=-1, keepdims=True) + EPS) * g


def _gelu_tanh(x):
    c = math.sqrt(2.0 / math.pi)
    half = 0.5 * x
    return half + half * jnp.tanh(x * (c + (c * 0.044715) * (x * x)))


def _block_diag_parts(a):
    lo = lax.broadcasted_iota(jnp.int32, a.shape, 1) < HEAD_DIM
    swapped = pltpu.roll(a, HEAD_DIM, 1)
    zero = jnp.zeros_like(a)
    parts = (jnp.where(lo, a, zero), jnp.where(lo, zero, swapped),
             jnp.where(lo, swapped, zero), jnp.where(lo, zero, a))
    return [p.astype(BF16) for p in parts]


def _conv_mix_block(glu_view, r0, rows, shift_buf, cw_ref, cb_ref, lng_ref, lnb_ref, goc_ref):
    first = CONV_PAD - (CONV_WIDTH - 1)
    span = rows + CONV_PAD - SUBLANES
    for a in range(1, SUBLANES):
        shift_buf[a - 1, 0:span, :] = glu_view[r0 + a:r0 + a + span, :]
    acc = jnp.broadcast_to(cb_ref[...], (rows, D_CONV))
    for j in range(CONV_WIDTH):
        a, base = (first + j) % SUBLANES, (first + j) // SUBLANES * SUBLANES
        if a == 0:
            src = glu_view[r0 + base:r0 + base + rows, :]
        else:
            src = shift_buf[a - 1, base:base + rows, :]
        acc = acc + cw_ref[j:j + 1, :] * src
    mu = jnp.mean(acc, axis=-1, keepdims=True)
    xc = acc - mu
    y = xc * lax.rsqrt(jnp.mean(xc * xc, axis=-1, keepdims=True) + EPS) * lng_ref[...] + lnb_ref[...]
    cy = y * jax.nn.sigmoid(y)
    return _rms(cy, goc_ref[...])


def _attn_block(q_ref, qrow, rows, kget, vget, valid, sinks_ref, out_ref):
    for kv_head in range(N_KV_HEADS):
        pairs = (2 * kv_head, 2 * kv_head + 1)
        q = jnp.concatenate([q_ref[qrow:qrow + rows, LANES * p:LANES * (p + 1)] for p in pairs], axis=0)
        pv, inv = [], []
        for half in range(2):
            s = _dot_t(q, kget(2 * kv_head + half))
            e_parts, inv_half = [], []
            for i, p in enumerate(pairs):
                sp = jnp.where(valid, s[i * rows:(i + 1) * rows, :], NEG_INF)
                sink = sinks_ref[2 * p + half]
                m = jnp.maximum(jnp.max(sp, axis=-1, keepdims=True), sink)
                e = jnp.exp(sp - m)
                inv_half.append(1.0 / (jnp.sum(e, axis=-1, keepdims=True) + jnp.exp(sink - m)))
                e_parts.append(e.astype(BF16))
            pv.append(_dot(jnp.concatenate(e_parts, axis=0), vget(2 * kv_head + half)))
            inv.append(inv_half)
        for i, p in enumerate(pairs):
            rs = slice(i * rows, (i + 1) * rows)
            out_ref[qrow:qrow + rows, LANES * p:LANES * (p + 1)] = pv[0][rs] * inv[0][i] + pv[1][rs] * inv[1][i]


def _sample_attention(q_ref, kbd, vbd, s_buf, e_buf, sinks_ref, out_ref, streams, seq, n_keys):
    valid = lax.broadcasted_iota(jnp.int32, (seq, KEY_BLOCK), 1) < n_keys
    units = [(kv_head, half) for kv_head in range(N_KV_HEADS) for half in range(2)]
    for s in range(streams):
        row = s * seq
        for kv_head in range(N_KV_HEADS):
            q = jnp.concatenate([q_ref[row:row + seq, LANES * p:LANES * (p + 1)]
                                 for p in (2 * kv_head, 2 * kv_head + 1)], axis=0)
            for half in range(2):
                s_buf[s, 2 * kv_head + half] = _dot_t(q, kbd[2 * kv_head + half, s])
    for s in range(streams):
        for kv_head, half in units:
            scores = s_buf[s, 2 * kv_head + half]
            for i, p in enumerate((2 * kv_head, 2 * kv_head + 1)):
                sp = jnp.where(valid, scores[i * seq:(i + 1) * seq, :], NEG_INF)
                sink = sinks_ref[2 * p + half]
                m = jnp.maximum(jnp.max(sp, axis=-1, keepdims=True), sink)
                e = jnp.exp(sp - m)
                denom = jnp.sum(e, axis=-1, keepdims=True) + jnp.exp(sink - m)
                e_buf[s, 2 * kv_head + half, i * seq:(i + 1) * seq, :] = (e * (1.0 / denom)).astype(BF16)
    for s in range(streams):
        row = s * seq
        for kv_head in range(N_KV_HEADS):
            pv = [_dot(e_buf[s, 2 * kv_head + half], vbd[2 * kv_head + half, s]) for half in range(2)]
            for i, p in enumerate((2 * kv_head, 2 * kv_head + 1)):
                rs = slice(i * seq, (i + 1) * seq)
                out_ref[row:row + seq, LANES * p:LANES * (p + 1)] = pv[0][rs] + pv[1][rs]


def _ffn_up(c, h2, gate_buf, val_buf, w_up_ref):
    segments, seg_rows = gate_buf.shape[0], gate_buf.shape[1] - FFN_PAD
    cols = slice(c * FFN_COLS, (c + 1) * FFN_COLS)
    gate_buf[:, FFN_PAD:, cols] = _dot(h2, w_up_ref[:, cols]).reshape(segments, seg_rows, FFN_COLS)
    val_buf[c % 2] = _dot(h2, w_up_ref[:, D_FF + c * FFN_COLS:D_FF + (c + 1) * FFN_COLS])


def _ffn_act(c, gate_buf, lag_buf, val_buf, act_buf, fcw_ref, fcb_ref):
    segments, seg_rows = gate_buf.shape[0], gate_buf.shape[1] - FFN_PAD
    cols = slice(c * FFN_COLS, (c + 1) * FFN_COLS)
    gc = fcb_ref[:, cols] + fcw_ref[FFN_CONV_WIDTH - 1:FFN_CONV_WIDTH, cols] * gate_buf[:, FFN_PAD:, cols]
    for lag in range(1, FFN_CONV_WIDTH):
        lag_buf[c % 2, lag - 1] = gate_buf[:, FFN_PAD - lag:FFN_PAD - lag + seg_rows, cols]
        j = FFN_CONV_WIDTH - 1 - lag
        gc = gc + fcw_ref[j:j + 1, cols] * lag_buf[c % 2, lag - 1]
    act = _gelu_tanh(gc).reshape(segments * seg_rows, FFN_COLS) * val_buf[c % 2]
    act_buf[c % 2] = act.astype(BF16)


def _ffn_down(c, act_buf, w_down_ref):
    return _dot(act_buf[c % 2], w_down_ref[c * FFN_COLS:(c + 1) * FFN_COLS, :])


def _prompt_kernel(x_ref, ga_ref, w_in_ref, cw_ref, cb_ref, lng_ref, lnb_ref, sinks_ref, goc_ref, goa_ref,
                   w_out_ref, gffn_ref, w_up_ref, fcw_ref, fcb_ref, w_down_ref, gfin_ref,
                   y_ref, convn_ref, kn_ref, vn_ref, ffnn_ref,
                   glu_buf, shift_buf, q_buf, kbd, vbd, ao_buf, mix_buf, x1_buf, gate_buf, lag_buf, val_buf, act_buf,
                   *, tiles_per_stream):
    i = pl.program_id(0)
    tile = x_ref.shape[1]
    mixer_starts_stream = i % tiles_per_stream == 0
    ffn_starts_stream = (i + tiles_per_stream - 1) % tiles_per_stream == 0

    @pl.when(i == 0)
    def _():
        x1_buf[...] = jnp.zeros(x1_buf.shape, F32)

    @pl.when(mixer_starts_stream)
    def _():
        glu_buf[0:CONV_PAD, :] = jnp.zeros((CONV_PAD, D_CONV), F32)
        for buf in (kbd, vbd):
            buf[:, 0:WINDOW, :] = jnp.zeros((4, WINDOW, LANES), BF16)

    @pl.when(jnp.logical_or(i == 0, ffn_starts_stream))
    def _():
        gate_buf[:, 0:FFN_PAD, :] = jnp.zeros((1, FFN_PAD, D_FF), F32)

    hb = _rms(x_ref[0], ga_ref[...]).astype(BF16)
    ab = _dot(hb, w_in_ref[:, 0:2 * D_CONV])
    glu_buf[CONV_PAD:, :] = ab[:, 0:D_CONV] * jax.nn.sigmoid(ab[:, D_CONV:])

    x1_prev = x1_buf[...]
    y_ref[0] = x1_prev
    h2 = _rms(x1_prev, gffn_ref[...]).astype(BF16)
    _ffn_up(0, h2, gate_buf, val_buf, w_up_ref)

    q_buf[...] = (_dot(hb, w_in_ref[:, 2 * D_CONV:2 * D_CONV + D_ATTN]) * SCALE).astype(BF16)
    kv = _dot(hb, w_in_ref[:, 2 * D_CONV + D_ATTN:])
    k = kv[:, 0:D_KV]
    v = kv[:, D_KV:]
    for part_index, part in enumerate(_block_diag_parts(k)):
        kbd[part_index, WINDOW:, :] = part
    for part_index, part in enumerate(_block_diag_parts(v)):
        vbd[part_index, WINDOW:, :] = part

    kn_ref[0] = k[tile - WINDOW:, :]
    vn_ref[0] = v[tile - WINDOW:, :]
    convn_ref[0] = glu_buf[CONV_PAD + tile - (CONV_WIDTH - 1):, :]

    def conv_task(r):
        r0 = r * CONV_ROWS
        n = _conv_mix_block(glu_buf, r0, CONV_ROWS, shift_buf.at[r % 2], cw_ref, cb_ref, lng_ref, lnb_ref, goc_ref)
        mix_buf[r0:r0 + CONV_ROWS, 0:D_CONV] = n.astype(BF16)

    chunk_shift = CHUNK.bit_length() - 1
    q_chunk = lax.broadcasted_iota(jnp.int32, (QUERY_BLOCK, KEY_BLOCK), 0) >> chunk_shift
    k_chunk = lax.broadcasted_iota(jnp.int32, (QUERY_BLOCK, KEY_BLOCK), 1) >> chunk_shift
    band = (k_chunk >= q_chunk) & (k_chunk <= q_chunk + WINDOW // CHUNK)

    def attn_task(b):
        r0 = b * QUERY_BLOCK
        valid = band & (k_chunk >= jnp.where(mixer_starts_stream, WINDOW // CHUNK, 0)) if b == 0 else band
        _attn_block(q_buf, r0, QUERY_BLOCK,
                    lambda i: kbd[i, r0:r0 + KEY_BLOCK, :],
                    lambda i: vbd[i, r0:r0 + KEY_BLOCK, :],
                    valid, sinks_ref, ao_buf)

    mixer_tasks = [functools.partial(conv_task, r) for r in range(tile // CONV_ROWS)]
    mixer_tasks += [functools.partial(attn_task, b) for b in range(tile // QUERY_BLOCK)]
    mixer_tasks = [mixer_tasks[j] for j in MIXER_ORDER]
    for c in range(N_FFN_PASSES):
        if c + 1 < N_FFN_PASSES:
            _ffn_up(c + 1, h2, gate_buf, val_buf, w_up_ref)
        _ffn_act(c, gate_buf, lag_buf, val_buf, act_buf, fcw_ref, fcb_ref)
        lo = len(mixer_tasks) * c // N_FFN_PASSES
        hi = len(mixer_tasks) * (c + 1) // N_FFN_PASSES
        for task in mixer_tasks[lo:hi]:
            task()
        if c == CONV_PROJECTION_PASS:
            x1_buf[...] = x_ref[0] + _dot(mix_buf[:, 0:D_CONV], w_out_ref[0:D_CONV, :])
        y_ref[0] += _ffn_down(c, act_buf, w_down_ref)

    glu_buf[0:CONV_PAD, :] = glu_buf[tile:, :]
    for buf in (kbd, vbd):
        buf[:, 0:WINDOW, :] = buf[:, tile:, :]
    mix_buf[:, D_CONV:] = _rms(ao_buf[...], goa_ref[...]).astype(BF16)

    ffnn_ref[...] = gate_buf[:, FFN_PAD + tile - (FFN_CONV_WIDTH - 1):, :]
    gate_buf[:, 0:FFN_PAD, :] = gate_buf[:, tile:, :]
    y_ref[0] = _rms(y_ref[0], gfin_ref[...])

    x1_buf[...] += _dot(mix_buf[:, D_CONV:], w_out_ref[D_CONV:, :])


def _sample_kernel(x_ref, cconv_ref, ck_ref, cv_ref, cffn_ref,
                   ga_ref, w_in_ref, cw_ref, cb_ref, lng_ref, lnb_ref, sinks_ref, goc_ref, goa_ref,
                   w_out_ref, gffn_ref, w_up_ref, fcw_ref, fcb_ref, w_down_ref, gfin_ref,
                   y_ref, convn_ref, kn_ref, vn_ref, ffnn_ref,
                   glu_buf, shift_buf, q_buf, kbd, vbd, s_buf, e_buf, ao_buf, mix_buf, x1_buf, gate_buf, lag_buf,
                   val_buf, act_buf):
    streams, seq = glu_buf.shape[0], glu_buf.shape[1] - CONV_PAD
    n_keys = WINDOW + seq

    hb = _rms(x_ref[...], ga_ref[...]).astype(BF16)
    ab = _dot(hb, w_in_ref[:, 0:2 * D_CONV])
    glu = ab[:, 0:D_CONV] * jax.nn.sigmoid(ab[:, D_CONV:])
    glu_buf[:, 0:CONV_PAD, :] = cconv_ref[...]
    glu_buf[:, CONV_PAD:, :] = glu.reshape(streams, seq, D_CONV)
    convn_ref[...] = glu_buf[:, CONV_PAD + seq - (CONV_WIDTH - 1):, :]
    q_buf[...] = (_dot(hb, w_in_ref[:, 2 * D_CONV:2 * D_CONV + D_ATTN]) * SCALE).astype(BF16)
    kv = _dot(hb, w_in_ref[:, 2 * D_CONV + D_ATTN:])
    for cache_ref, new_ref, buf, lanes in ((ck_ref, kn_ref, kbd, slice(0, D_KV)),
                                           (cv_ref, vn_ref, vbd, slice(D_KV, 2 * D_KV))):
        hist = cache_ref[...]
        new = kv[:, lanes].reshape(streams, seq, D_KV)
        new_ref[:, 0:WINDOW - seq, :] = hist[:, seq:, :]
        new_ref[:, WINDOW - seq:, :] = new
        for i, part in enumerate(_block_diag_parts(hist.reshape(streams * WINDOW, D_KV))):
            buf[i, :, 0:WINDOW, :] = part.reshape(streams, WINDOW, LANES)
        for i, part in enumerate(_block_diag_parts(kv[:, lanes])):
            buf[i, :, WINDOW:n_keys, :] = part.reshape(streams, seq, LANES)
        buf[:, :, n_keys:, :] = jnp.zeros((4, streams, KEY_BLOCK - n_keys, LANES), BF16)

    _sample_attention(q_buf, kbd, vbd, s_buf, e_buf, sinks_ref, ao_buf, streams, seq, n_keys)
    for s in range(streams):
        row = s * seq
        n = _conv_mix_block(glu_buf.at[s], 0, seq, shift_buf.at[s % 2], cw_ref, cb_ref, lng_ref, lnb_ref, goc_ref)
        mix_buf[row:row + seq, 0:D_CONV] = n.astype(BF16)
    mix_buf[:, D_CONV:] = _rms(ao_buf[...], goa_ref[...]).astype(BF16)

    x1 = x_ref[...] + _dot(mix_buf[...], w_out_ref[...])
    x1_buf[...] = x1
    h2 = _rms(x1, gffn_ref[...]).astype(BF16)

    gate_buf[:, 0:FFN_PAD, :] = cffn_ref[...]
    _ffn_up(0, h2, gate_buf, val_buf, w_up_ref)
    for c in range(N_FFN_PASSES):
        if c + 1 < N_FFN_PASSES:
            _ffn_up(c + 1, h2, gate_buf, val_buf, w_up_ref)
        _ffn_act(c, gate_buf, lag_buf, val_buf, act_buf, fcw_ref, fcb_ref)
        x1_buf[...] += _ffn_down(c, act_buf, w_down_ref)
    ffnn_ref[...] = gate_buf[:, FFN_PAD + seq - (FFN_CONV_WIDTH - 1):, :]
    y_ref[...] = _rms(x1_buf[...], gfin_ref[...])


def _resident(shape):
    zeros = (0,) * len(shape)
    return pl.BlockSpec(shape, lambda *_: zeros, pipeline_mode=pl.Buffered(1))


def _weight_specs():
    row = lambda n: _resident((1, n))
    return [
        row(D_MODEL),
        _resident((D_MODEL, D_IN)),
        _resident((CONV_WIDTH, D_CONV)),
        row(D_CONV), row(D_CONV), row(D_CONV),
        pl.BlockSpec(memory_space=pltpu.SMEM),
        row(D_CONV), row(D_ATTN),
        _resident((D_MIX, D_MODEL)),
        row(D_MODEL),
        _resident((D_MODEL, 2 * D_FF)),
        _resident((FFN_CONV_WIDTH, D_FF)),
        row(D_FF),
        _resident((D_FF, D_MODEL)),
        row(D_MODEL),
    ]


def _shift_scratch(rows):
    return pltpu.VMEM((2, SUBLANES - 1, rows + CONV_PAD - SUBLANES, D_CONV), F32)


def _prompt_call(x, weights):
    batch, seq, _ = x.shape
    tile = PROMPT_TILE
    per_stream = seq // tile
    n_tiles = batch * per_stream
    mixer_tile = lambda i: jnp.minimum(i, n_tiles - 1)
    ffn_tile = lambda i: jnp.maximum(i - 1, 0)
    mixer_out = lambda shape: pl.BlockSpec((1,) + shape, lambda i: (mixer_tile(i) // per_stream, 0, 0))
    return pl.pallas_call(
        functools.partial(_prompt_kernel, tiles_per_stream=per_stream),
        grid=(n_tiles + 1,),
        in_specs=[pl.BlockSpec((1, tile, D_MODEL),
                               lambda i: (mixer_tile(i) // per_stream, mixer_tile(i) % per_stream, 0))]
        + _weight_specs(),
        out_specs=[
            pl.BlockSpec((1, tile, D_MODEL), lambda i: (ffn_tile(i) // per_stream, ffn_tile(i) % per_stream, 0)),
            mixer_out((CONV_WIDTH - 1, D_CONV)),
            mixer_out((WINDOW, D_KV)),
            mixer_out((WINDOW, D_KV)),
            pl.BlockSpec((1, FFN_CONV_WIDTH - 1, D_FF), lambda i: (ffn_tile(i) // per_stream, 0, 0)),
        ],
        out_shape=[
            jax.ShapeDtypeStruct((batch, seq, D_MODEL), F32),
            jax.ShapeDtypeStruct((batch, CONV_WIDTH - 1, D_CONV), F32),
            jax.ShapeDtypeStruct((batch, WINDOW, D_KV), F32),
            jax.ShapeDtypeStruct((batch, WINDOW, D_KV), F32),
            jax.ShapeDtypeStruct((batch, FFN_CONV_WIDTH - 1, D_FF), F32),
        ],
        scratch_shapes=[
            pltpu.VMEM((CONV_PAD + tile, D_CONV), F32),
            _shift_scratch(CONV_ROWS),
            pltpu.VMEM((tile, D_ATTN), BF16),
            pltpu.VMEM((4, WINDOW + tile, LANES), BF16),
            pltpu.VMEM((4, WINDOW + tile, LANES), BF16),
            pltpu.VMEM((tile, D_ATTN), F32),
            pltpu.VMEM((tile, D_MIX), BF16),
            pltpu.VMEM((tile, D_MODEL), F32),
            pltpu.VMEM((1, FFN_PAD + tile, D_FF), F32),
            pltpu.VMEM((2, FFN_CONV_WIDTH - 1, 1, tile, FFN_COLS), F32),
            pltpu.VMEM((2, tile, FFN_COLS), F32),
            pltpu.VMEM((2, tile, FFN_COLS), BF16),
        ],
        compiler_params=pltpu.CompilerParams(
            dimension_semantics=("arbitrary",),
            vmem_limit_bytes=VMEM_LIMIT_BYTES),
        name="prompt_layer",
    )(x, *weights)


def _sample_call(x, cconv, ck, cv, cffn, weights):
    n_streams, seq, _ = x.shape
    sb = SAMPLE_STREAMS
    rows = sb * seq
    per_stream = lambda shape: pl.BlockSpec((sb,) + shape, lambda i: (i, 0, 0))
    return pl.pallas_call(
        _sample_kernel,
        grid=(n_streams // sb,),
        in_specs=[
            pl.BlockSpec((rows, D_MODEL), lambda i: (i, 0)),
            per_stream((CONV_PAD, D_CONV)),
            per_stream((WINDOW, D_KV)),
            per_stream((WINDOW, D_KV)),
            per_stream((FFN_PAD, D_FF)),
        ] + _weight_specs(),
        out_specs=[
            pl.BlockSpec((rows, D_MODEL), lambda i: (i, 0)),
            per_stream((CONV_WIDTH - 1, D_CONV)),
            per_stream((WINDOW, D_KV)),
            per_stream((WINDOW, D_KV)),
            per_stream((FFN_CONV_WIDTH - 1, D_FF)),
        ],
        out_shape=[
            jax.ShapeDtypeStruct((n_streams * seq, D_MODEL), F32),
            jax.ShapeDtypeStruct((n_streams, CONV_WIDTH - 1, D_CONV), F32),
            jax.ShapeDtypeStruct((n_streams, WINDOW, D_KV), F32),
            jax.ShapeDtypeStruct((n_streams, WINDOW, D_KV), F32),
            jax.ShapeDtypeStruct((n_streams, FFN_CONV_WIDTH - 1, D_FF), F32),
        ],
        scratch_shapes=[
            pltpu.VMEM((sb, CONV_PAD + seq, D_CONV), F32),
            _shift_scratch(seq),
            pltpu.VMEM((rows, D_ATTN), BF16),
            pltpu.VMEM((4, sb, KEY_BLOCK, LANES), BF16),
            pltpu.VMEM((4, sb, KEY_BLOCK, LANES), BF16),
            pltpu.VMEM((sb, 4, 2 * seq, KEY_BLOCK), F32),
            pltpu.VMEM((sb, 4, 2 * seq, KEY_BLOCK), BF16),
            pltpu.VMEM((rows, D_ATTN), F32),
            pltpu.VMEM((rows, D_MIX), BF16),
            pltpu.VMEM((rows, D_MODEL), F32),
            pltpu.VMEM((sb, FFN_PAD + seq, D_FF), F32),
            pltpu.VMEM((2, FFN_CONV_WIDTH - 1, sb, seq, FFN_COLS), F32),
            pltpu.VMEM((2, rows, FFN_COLS), F32),
            pltpu.VMEM((2, rows, FFN_COLS), BF16),
        ],
        compiler_params=pltpu.CompilerParams(
            dimension_semantics=("arbitrary",),
            vmem_limit_bytes=VMEM_LIMIT_BYTES),
        name="sample_layer",
    )(x.reshape(n_streams * seq, D_MODEL), cconv, ck, cv, cffn, *weights)


def kernel(x_prompt, x_sample, cache_conv, cache_k, cache_v, cache_ffn_conv, g_attn_norm, w_in, conv_w, conv_b,
           conv_ln_g, conv_ln_b, sinks, g_out_conv, g_out_attn, w_out, g_ffn_norm, w_up, ffn_conv_w, ffn_conv_b,
           w_down, g_final):
    depth = w_in.shape[0]
    assert depth == 1, "single trunk layer"
    batch, seq, _ = x_prompt.shape
    n_streams, dec_seq, _ = x_sample.shape
    assert seq % PROMPT_TILE == 0 and PROMPT_TILE % QUERY_BLOCK == 0 and PROMPT_TILE % CONV_ROWS == 0
    assert n_streams % SAMPLE_STREAMS == 0 and dec_seq % (2 * SUBLANES) == 0
    assert CONV_WIDTH - 1 <= dec_seq <= KEY_BLOCK - WINDOW

    row = lambda a: a.reshape(1, -1)
    weights = (
        row(g_attn_norm[0]), w_in[0].astype(BF16), conv_w[0], row(conv_b[0]), row(conv_ln_g[0]),
        row(conv_ln_b[0]), sinks[0], row(g_out_conv[0]), row(g_out_attn[0]), w_out[0].astype(BF16),
        row(g_ffn_norm[0]), w_up[0].astype(BF16), ffn_conv_w[0], row(ffn_conv_b[0]), w_down[0].astype(BF16),
        row(g_final),
    )

    y_p, conv_p, k_p, v_p, ffn_p = _prompt_call(x_prompt, weights)

    cconv = jnp.pad(cache_conv[0], ((0, 0), (CONV_PAD - (CONV_WIDTH - 1), 0), (0, 0)))
    cffn = jnp.pad(cache_ffn_conv[0], ((0, 0), (FFN_PAD - (FFN_CONV_WIDTH - 1), 0), (0, 0)))
    ck = cache_k[0].reshape(n_streams, WINDOW, D_KV)
    cv = cache_v[0].reshape(n_streams, WINDOW, D_KV)
    y_s, conv_s, k_s, v_s, ffn_s = _sample_call(x_sample, cconv, ck, cv, cffn, weights)

    heads = lambda a: a.reshape(depth, a.shape[0], WINDOW, N_KV_HEADS, HEAD_DIM)
    return (y_p, y_s.reshape(n_streams, dec_seq, D_MODEL),
            conv_p[None], heads(k_p), heads(v_p), ffn_p[None],
            conv_s[None], heads(k_s), heads(v_s), ffn_s[None])
```

```python
import functools
import math

import jax
import jax.numpy as jnp
from jax import lax
from jax.experimental import pallas as pl
from jax.experimental.pallas import tpu as pltpu

D_MODEL = 1024
CHUNK = 64
D_CONV = 512
CONV_WIDTH = 31
N_Q_HEADS = 8
N_KV_HEADS = 2
HEAD_DIM = 64
D_ATTN = N_Q_HEADS * HEAD_DIM
D_KV = N_KV_HEADS * HEAD_DIM
WINDOW = 128
D_MIX = D_CONV + D_ATTN
D_IN = 2 * D_CONV + D_ATTN + 2 * D_KV
D_FF = 2816
FFN_CONV_WIDTH = 3
EPS = 1e-6
NEG_INF = -1e30
SCALE = HEAD_DIM ** -0.5

SUBLANES = 8
LANES = 128
MXU_COLS = 256
KEY_BLOCK = MXU_COLS
QUERY_BLOCK = KEY_BLOCK - WINDOW
CONV_PAD = 32
CONV_ROWS = 64
FFN_PAD = SUBLANES
FFN_COLS = MXU_COLS
N_FFN_PASSES = D_FF // FFN_COLS
PROMPT_TILE = 512
SAMPLE_STREAMS = 8
VMEM_LIMIT_BYTES = 58 * 1024 * 1024

BF16 = jnp.bfloat16
F32 = jnp.float32


def _dot(a, b):
    return jnp.dot(a, b, preferred_element_type=F32)


def _dot_t(a, b):
    return lax.dot_general(a, b, (((1,), (1,)), ((), ())), preferred_element_type=F32)


def _rms(x, g):
    return x * lax.rsqrt(jnp.mean(x * x, axis=-1, keepdims=True) + EPS) * g


def _gelu_tanh(x):
    c = math.sqrt(2.0 / math.pi)
    half = 0.5 * x
    return half + half * jnp.tanh(x * (c + (c * 0.044715) * (x * x)))


def _block_diag_parts(a):
    lo = lax.broadcasted_iota(jnp.int32, a.shape, 1) < HEAD_DIM
    swapped = pltpu.roll(a, HEAD_DIM, 1)
    zero = jnp.zeros_like(a)
    parts = (jnp.where(lo, a, zero), jnp.where(lo, zero, swapped),
             jnp.where(lo, swapped, zero), jnp.where(lo, zero, a))
    return [p.astype(BF16) for p in parts]


def _conv_mix_block(glu_view, r0, rows, shift_buf, cw_ref, cb_ref, lng_ref, lnb_ref, goc_ref):
    first = CONV_PAD - (CONV_WIDTH - 1)
    span = rows + CONV_PAD - SUBLANES
    for a in range(1, SUBLANES):
        shift_buf[a - 1, 0:span, :] = glu_view[r0 + a:r0 + a + span, :]
    acc = jnp.broadcast_to(cb_ref[...], (rows, D_CONV))
    for j in range(CONV_WIDTH):
        a, base = (first + j) % SUBLANES, (first + j) // SUBLANES * SUBLANES
        if a == 0:
            src = glu_view[r0 + base:r0 + base + rows, :]
        else:
            src = shift_buf[a - 1, base:base + rows, :]
        acc = acc + cw_ref[j:j + 1, :] * src
    mu = jnp.mean(acc, axis=-1, keepdims=True)
    xc = acc - mu
    y = xc * lax.rsqrt(jnp.mean(xc * xc, axis=-1, keepdims=True) + EPS) * lng_ref[...] + lnb_ref[...]
    cy = y * jax.nn.sigmoid(y)
    return _rms(cy, goc_ref[...])


def _attn_block(q_ref, qrow, rows, kget, vget, valid, sinks_ref, out_ref):
    for kv_head in range(N_KV_HEADS):
        pairs = (2 * kv_head, 2 * kv_head + 1)
        q = jnp.concatenate([q_ref[qrow:qrow + rows, LANES * p:LANES * (p + 1)] for p in pairs], axis=0)
        pv, inv = [], []
        for half in range(2):
            s = _dot_t(q, kget(2 * kv_head + half))
            e_parts, inv_half = [], []
            for i, p in enumerate(pairs):
                sp = jnp.where(valid, s[i * rows:(i + 1) * rows, :], NEG_INF)
                sink = sinks_ref[2 * p + half]
                m = jnp.maximum(jnp.max(sp, axis=-1, keepdims=True), sink)
                e = jnp.exp(sp - m)
                inv_half.append(1.0 / (jnp.sum(e, axis=-1, keepdims=True) + jnp.exp(sink - m)))
                e_parts.append(e.astype(BF16))
            pv.append(_dot(jnp.concatenate(e_parts, axis=0), vget(2 * kv_head + half)))
            inv.append(inv_half)
        for i, p in enumerate(pairs):
            rs = slice(i * rows, (i + 1) * rows)
            out_ref[qrow:qrow + rows, LANES * p:LANES * (p + 1)] = pv[0][rs] * inv[0][i] + pv[1][rs] * inv[1][i]


def _sample_attention(q_ref, kbd, vbd, s_buf, e_buf, sinks_ref, out_ref, streams, seq, n_keys):
    valid = lax.broadcasted_iota(jnp.int32, (seq, KEY_BLOCK), 1) < n_keys
    units = [(kv_head, half) for kv_head in range(N_KV_HEADS) for half in range(2)]
    for s in range(streams):
        row = s * seq
        for kv_head in range(N_KV_HEADS):
            q = jnp.concatenate([q_ref[row:row + seq, LANES * p:LANES * (p + 1)]
                                 for p in (2 * kv_head, 2 * kv_head + 1)], axis=0)
            for half in range(2):
                s_buf[s, 2 * kv_head + half] = _dot_t(q, kbd[2 * kv_head + half, s])
    for s in range(streams):
        for kv_head, half in units:
            scores = s_buf[s, 2 * kv_head + half]
            for i, p in enumerate((2 * kv_head, 2 * kv_head + 1)):
                sp = jnp.where(valid, scores[i * seq:(i + 1) * seq, :], NEG_INF)
                sink = sinks_ref[2 * p + half]
                m = jnp.maximum(jnp.max(sp, axis=-1, keepdims=True), sink)
                e = jnp.exp(sp - m)
                denom = jnp.sum(e, axis=-1, keepdims=True) + jnp.exp(sink - m)
                e_buf[s, 2 * kv_head + half, i * seq:(i + 1) * seq, :] = (e * (1.0 / denom)).astype(BF16)
    for s in range(streams):
        row = s * seq
        for kv_head in range(N_KV_HEADS):
            pv = [_dot(e_buf[s, 2 * kv_head + half], vbd[2 * kv_head + half, s]) for half in range(2)]
            for i, p in enumerate((2 * kv_head, 2 * kv_head + 1)):
                rs = slice(i * seq, (i + 1) * seq)
                out_ref[row:row + seq, LANES * p:LANES * (p + 1)] = pv[0][rs] + pv[1][rs]


def _ffn_up(c, h2, gate_buf, val_buf, w_up_ref):
    segments, seg_rows = gate_buf.shape[0], gate_buf.shape[1] - FFN_PAD
    cols = slice(c * FFN_COLS, (c + 1) * FFN_COLS)
    gate_buf[:, FFN_PAD:, cols] = _dot(h2, w_up_ref[:, cols]).reshape(segments, seg_rows, FFN_COLS)
    val_buf[c % 2] = _dot(h2, w_up_ref[:, D_FF + c * FFN_COLS:D_FF + (c + 1) * FFN_COLS])


def _ffn_act(c, gate_buf, lag_buf, val_buf, act_buf, fcw_ref, fcb_ref):
    segments, seg_rows = gate_buf.shape[0], gate_buf.shape[1] - FFN_PAD
    cols = slice(c * FFN_COLS, (c + 1) * FFN_COLS)
    gc = fcb_ref[:, cols] + fcw_ref[FFN_CONV_WIDTH - 1:FFN_CONV_WIDTH, cols] * gate_buf[:, FFN_PAD:, cols]
    for lag in range(1, FFN_CONV_WIDTH):
        lag_buf[c % 2, lag - 1] = gate_buf[:, FFN_PAD - lag:FFN_PAD - lag + seg_rows, cols]
        j = FFN_CONV_WIDTH - 1 - lag
        gc = gc + fcw_ref[j:j + 1, cols] * lag_buf[c % 2, lag - 1]
    act = _gelu_tanh(gc).reshape(segments * seg_rows, FFN_COLS) * val_buf[c % 2]
    act_buf[:, cols] = act.astype(BF16)


def _ffn_hidden(h2, gate_buf, lag_buf, val_buf, act_buf, w_up_ref, fcw_ref, fcb_ref):
    _ffn_up(0, h2, gate_buf, val_buf, w_up_ref)
    for c in range(N_FFN_PASSES):
        if c + 1 < N_FFN_PASSES:
            _ffn_up(c + 1, h2, gate_buf, val_buf, w_up_ref)
        _ffn_act(c, gate_buf, lag_buf, val_buf, act_buf, fcw_ref, fcb_ref)


def _prompt_kernel(x_ref, ga_ref, w_in_ref, cw_ref, cb_ref, lng_ref, lnb_ref, sinks_ref, goc_ref, goa_ref,
                   w_out_ref, gffn_ref, w_up_ref, fcw_ref, fcb_ref, w_down_ref, gfin_ref,
                   y_ref, convn_ref, kn_ref, vn_ref, ffnn_ref,
                   glu_buf, shift_buf, q_buf, kbd, vbd, ao_buf, mix_buf, x1_buf, gate_buf, lag_buf, val_buf, act_buf,
                   *, tiles_per_stream):
    i = pl.program_id(0)
    tile = x_ref.shape[1]
    mixer_starts_stream = i % tiles_per_stream == 0
    ffn_starts_stream = (i + tiles_per_stream - 1) % tiles_per_stream == 0

    @pl.when(i == 0)
    def _():
        x1_buf[...] = jnp.zeros(x1_buf.shape, F32)

    @pl.when(mixer_starts_stream)
    def _():
        glu_buf[0:CONV_PAD, :] = jnp.zeros((CONV_PAD, D_CONV), F32)
        for buf in (kbd, vbd):
            buf[:, 0:WINDOW, :] = jnp.zeros((4, WINDOW, LANES), BF16)

    @pl.when(jnp.logical_or(i == 0, ffn_starts_stream))
    def _():
        gate_buf[:, 0:FFN_PAD, :] = jnp.zeros((1, FFN_PAD, D_FF), F32)

    hb = _rms(x_ref[0], ga_ref[...]).astype(BF16)
    ab = _dot(hb, w_in_ref[:, 0:2 * D_CONV])
    glu_buf[CONV_PAD:, :] = ab[:, 0:D_CONV] * jax.nn.sigmoid(ab[:, D_CONV:])

    q_buf[...] = (_dot(hb, w_in_ref[:, 2 * D_CONV:2 * D_CONV + D_ATTN]) * SCALE).astype(BF16)
    kv = _dot(hb, w_in_ref[:, 2 * D_CONV + D_ATTN:])
    k = kv[:, 0:D_KV]
    v = kv[:, D_KV:]
    for part_index, part in enumerate(_block_diag_parts(k)):
        kbd[part_index, WINDOW:, :] = part
    for part_index, part in enumerate(_block_diag_parts(v)):
        vbd[part_index, WINDOW:, :] = part

    kn_ref[0] = k[tile - WINDOW:, :]
    vn_ref[0] = v[tile - WINDOW:, :]
    convn_ref[0] = glu_buf[CONV_PAD + tile - (CONV_WIDTH - 1):, :]

    def conv_task(r):
        r0 = r * CONV_ROWS
        n = _conv_mix_block(glu_buf, r0, CONV_ROWS, shift_buf.at[r % 2], cw_ref, cb_ref, lng_ref, lnb_ref, goc_ref)
        mix_buf[r0:r0 + CONV_ROWS, 0:D_CONV] = n.astype(BF16)

    chunk_shift = CHUNK.bit_length() - 1
    q_chunk = lax.broadcasted_iota(jnp.int32, (QUERY_BLOCK, KEY_BLOCK), 0) >> chunk_shift
    k_chunk = lax.broadcasted_iota(jnp.int32, (QUERY_BLOCK, KEY_BLOCK), 1) >> chunk_shift
    band = (k_chunk >= q_chunk) & (k_chunk <= q_chunk + WINDOW // CHUNK)

    def attn_task(b):
        r0 = b * QUERY_BLOCK
        valid = band & (k_chunk >= jnp.where(mixer_starts_stream, WINDOW // CHUNK, 0)) if b == 0 else band
        _attn_block(q_buf, r0, QUERY_BLOCK,
                    lambda i: kbd[i, r0:r0 + KEY_BLOCK, :],
                    lambda i: vbd[i, r0:r0 + KEY_BLOCK, :],
                    valid, sinks_ref, ao_buf)

    x1_prev = x1_buf[...]
    h2 = _rms(x1_prev, gffn_ref[...]).astype(BF16)
    _ffn_hidden(h2, gate_buf, lag_buf, val_buf, act_buf, w_up_ref, fcw_ref, fcb_ref)
    ffnn_ref[...] = gate_buf[:, FFN_PAD + tile - (FFN_CONV_WIDTH - 1):, :]
    gate_buf[:, 0:FFN_PAD, :] = gate_buf[:, tile:, :]
    y_ref[0] = _rms(x1_prev + _dot(act_buf[...], w_down_ref[...]), gfin_ref[...])

    for r in range(tile // CONV_ROWS):
        conv_task(r)
    for b in range(tile // QUERY_BLOCK):
        attn_task(b)
    glu_buf[0:CONV_PAD, :] = glu_buf[tile:, :]
    for buf in (kbd, vbd):
        buf[:, 0:WINDOW, :] = buf[:, tile:, :]
    mix_buf[:, D_CONV:] = _rms(ao_buf[...], goa_ref[...]).astype(BF16)

    x1_buf[...] = x_ref[0] + _dot(mix_buf[...], w_out_ref[...])


def _sample_kernel(x_ref, cconv_ref, ck_ref, cv_ref, cffn_ref,
                   ga_ref, w_in_ref, cw_ref, cb_ref, lng_ref, lnb_ref, sinks_ref, goc_ref, goa_ref,
                   w_out_ref, gffn_ref, w_up_ref, fcw_ref, fcb_ref, w_down_ref, gfin_ref,
                   y_ref, convn_ref, kn_ref, vn_ref, ffnn_ref,
                   glu_buf, shift_buf, q_buf, kbd, vbd, s_buf, e_buf, ao_buf, mix_buf, x1_buf, gate_buf, lag_buf,
                   val_buf, act_buf):
    streams, seq = glu_buf.shape[0], glu_buf.shape[1] - CONV_PAD
    n_keys = WINDOW + seq

    hb = _rms(x_ref[...], ga_ref[...]).astype(BF16)
    ab = _dot(hb, w_in_ref[:, 0:2 * D_CONV])
    glu = ab[:, 0:D_CONV] * jax.nn.sigmoid(ab[:, D_CONV:])
    glu_buf[:, 0:CONV_PAD, :] = cconv_ref[...]
    glu_buf[:, CONV_PAD:, :] = glu.reshape(streams, seq, D_CONV)
    convn_ref[...] = glu_buf[:, CONV_PAD + seq - (CONV_WIDTH - 1):, :]
    q_buf[...] = (_dot(hb, w_in_ref[:, 2 * D_CONV:2 * D_CONV + D_ATTN]) * SCALE).astype(BF16)
    kv = _dot(hb, w_in_ref[:, 2 * D_CONV + D_ATTN:])
    for cache_ref, new_ref, buf, lanes in ((ck_ref, kn_ref, kbd, slice(0, D_KV)),
                                           (cv_ref, vn_ref, vbd, slice(D_KV, 2 * D_KV))):
        hist = cache_ref[...]
        new = kv[:, lanes].reshape(streams, seq, D_KV)
        new_ref[:, 0:WINDOW - seq, :] = hist[:, seq:, :]
        new_ref[:, WINDOW - seq:, :] = new
        for i, part in enumerate(_block_diag_parts(hist.reshape(streams * WINDOW, D_KV))):
            buf[i, :, 0:WINDOW, :] = part.reshape(streams, WINDOW, LANES)
        for i, part in enumerate(_block_diag_parts(kv[:, lanes])):
            buf[i, :, WINDOW:n_keys, :] = part.reshape(streams, seq, LANES)
        buf[:, :, n_keys:, :] = jnp.zeros((4, streams, KEY_BLOCK - n_keys, LANES), BF16)

    _sample_attention(q_buf, kbd, vbd, s_buf, e_buf, sinks_ref, ao_buf, streams, seq, n_keys)
    for s in range(streams):
        row = s * seq
        n = _conv_mix_block(glu_buf.at[s], 0, seq, shift_buf.at[s % 2], cw_ref, cb_ref, lng_ref, lnb_ref, goc_ref)
        mix_buf[row:row + seq, 0:D_CONV] = n.astype(BF16)
    mix_buf[:, D_CONV:] = _rms(ao_buf[...], goa_ref[...]).astype(BF16)

    x1_buf[...] = x_ref[...] + _dot(mix_buf[...], w_out_ref[...])
    h2 = _rms(x1_buf[...], gffn_ref[...]).astype(BF16)

    gate_buf[:, 0:FFN_PAD, :] = cffn_ref[...]
    _ffn_hidden(h2, gate_buf, lag_buf, val_buf, act_buf, w_up_ref, fcw_ref, fcb_ref)
    ffnn_ref[...] = gate_buf[:, FFN_PAD + seq - (FFN_CONV_WIDTH - 1):, :]
    y_ref[...] = _rms(x1_buf[...] + _dot(act_buf[...], w_down_ref[...]), gfin_ref[...])


def _resident(shape):
    zeros = (0,) * len(shape)
    return pl.BlockSpec(shape, lambda *_: zeros, pipeline_mode=pl.Buffered(1))


def _weight_specs():
    row = lambda n: _resident((1, n))
    return [
        row(D_MODEL),
        _resident((D_MODEL, D_IN)),
        _resident((CONV_WIDTH, D_CONV)),
        row(D_CONV), row(D_CONV), row(D_CONV),
        pl.BlockSpec(memory_space=pltpu.SMEM),
        row(D_CONV), row(D_ATTN),
        _resident((D_MIX, D_MODEL)),
        row(D_MODEL),
        _resident((D_MODEL, 2 * D_FF)),
        _resident((FFN_CONV_WIDTH, D_FF)),
        row(D_FF),
        _resident((D_FF, D_MODEL)),
        row(D_MODEL),
    ]


def _shift_scratch(rows):
    return pltpu.VMEM((2, SUBLANES - 1, rows + CONV_PAD - SUBLANES, D_CONV), F32)


def _prompt_call(x, weights):
    batch, seq, _ = x.shape
    tile = PROMPT_TILE
    per_stream = seq // tile
    n_tiles = batch * per_stream
    mixer_tile = lambda i: jnp.minimum(i, n_tiles - 1)
    ffn_tile = lambda i: jnp.maximum(i - 1, 0)
    mixer_out = lambda shape: pl.BlockSpec((1,) + shape, lambda i: (mixer_tile(i) // per_stream, 0, 0))
    return pl.pallas_call(
        functools.partial(_prompt_kernel, tiles_per_stream=per_stream),
        grid=(n_tiles + 1,),
        in_specs=[pl.BlockSpec((1, tile, D_MODEL),
                               lambda i: (mixer_tile(i) // per_stream, mixer_tile(i) % per_stream, 0))]
        + _weight_specs(),
        out_specs=[
            pl.BlockSpec((1, tile, D_MODEL), lambda i: (ffn_tile(i) // per_stream, ffn_tile(i) % per_stream, 0)),
            mixer_out((CONV_WIDTH - 1, D_CONV)),
            mixer_out((WINDOW, D_KV)),
            mixer_out((WINDOW, D_KV)),
            pl.BlockSpec((1, FFN_CONV_WIDTH - 1, D_FF), lambda i: (ffn_tile(i) // per_stream, 0, 0)),
        ],
        out_shape=[
            jax.ShapeDtypeStruct((batch, seq, D_MODEL), F32),
            jax.ShapeDtypeStruct((batch, CONV_WIDTH - 1, D_CONV), F32),
            jax.ShapeDtypeStruct((batch, WINDOW, D_KV), F32),
            jax.ShapeDtypeStruct((batch, WINDOW, D_KV), F32),
            jax.ShapeDtypeStruct((batch, FFN_CONV_WIDTH - 1, D_FF), F32),
        ],
        scratch_shapes=[
            pltpu.VMEM((CONV_PAD + tile, D_CONV), F32),
            _shift_scratch(CONV_ROWS),
            pltpu.VMEM((tile, D_ATTN), BF16),
            pltpu.VMEM((4, WINDOW + tile, LANES), BF16),
            pltpu.VMEM((4, WINDOW + tile, LANES), BF16),
            pltpu.VMEM((tile, D_ATTN), F32),
            pltpu.VMEM((tile, D_MIX), BF16),
            pltpu.VMEM((tile, D_MODEL), F32),
            pltpu.VMEM((1, FFN_PAD + tile, D_FF), F32),
            pltpu.VMEM((2, FFN_CONV_WIDTH - 1, 1, tile, FFN_COLS), F32),
            pltpu.VMEM((2, tile, FFN_COLS), F32),
            pltpu.VMEM((tile, D_FF), BF16),
        ],
        compiler_params=pltpu.CompilerParams(
            dimension_semantics=("arbitrary",),
            vmem_limit_bytes=VMEM_LIMIT_BYTES),
        name="prompt_layer",
    )(x, *weights)


def _sample_call(x, cconv, ck, cv, cffn, weights):
    n_streams, seq, _ = x.shape
    sb = SAMPLE_STREAMS
    rows = sb * seq
    per_stream = lambda shape: pl.BlockSpec((sb,) + shape, lambda i: (i, 0, 0))
    return pl.pallas_call(
        _sample_kernel,
        grid=(n_streams // sb,),
        in_specs=[
            pl.BlockSpec((rows, D_MODEL), lambda i: (i, 0)),
            per_stream((CONV_PAD, D_CONV)),
            per_stream((WINDOW, D_KV)),
            per_stream((WINDOW, D_KV)),
            per_stream((FFN_PAD, D_FF)),
        ] + _weight_specs(),
        out_specs=[
            pl.BlockSpec((rows, D_MODEL), lambda i: (i, 0)),
            per_stream((CONV_WIDTH - 1, D_CONV)),
            per_stream((WINDOW, D_KV)),
            per_stream((WINDOW, D_KV)),
            per_stream((FFN_CONV_WIDTH - 1, D_FF)),
        ],
        out_shape=[
            jax.ShapeDtypeStruct((n_streams * seq, D_MODEL), F32),
            jax.ShapeDtypeStruct((n_streams, CONV_WIDTH - 1, D_CONV), F32),
            jax.ShapeDtypeStruct((n_streams, WINDOW, D_KV), F32),
            jax.ShapeDtypeStruct((n_streams, WINDOW, D_KV), F32),
            jax.ShapeDtypeStruct((n_streams, FFN_CONV_WIDTH - 1, D_FF), F32),
        ],
        scratch_shapes=[
            pltpu.VMEM((sb, CONV_PAD + seq, D_CONV), F32),
            _shift_scratch(seq),
            pltpu.VMEM((rows, D_ATTN), BF16),
            pltpu.VMEM((4, sb, KEY_BLOCK, LANES), BF16),
            pltpu.VMEM((4, sb, KEY_BLOCK, LANES), BF16),
            pltpu.VMEM((sb, 4, 2 * seq, KEY_BLOCK), F32),
            pltpu.VMEM((sb, 4, 2 * seq, KEY_BLOCK), BF16),
            pltpu.VMEM((rows, D_ATTN), F32),
            pltpu.VMEM((rows, D_MIX), BF16),
            pltpu.VMEM((rows, D_MODEL), F32),
            pltpu.VMEM((sb, FFN_PAD + seq, D_FF), F32),
            pltpu.VMEM((2, FFN_CONV_WIDTH - 1, sb, seq, FFN_COLS), F32),
            pltpu.VMEM((2, rows, FFN_COLS), F32),
            pltpu.VMEM((rows, D_FF), BF16),
        ],
        compiler_params=pltpu.CompilerParams(
            dimension_semantics=("arbitrary",),
            vmem_limit_bytes=VMEM_LIMIT_BYTES),
        name="sample_layer",
    )(x.reshape(n_streams * seq, D_MODEL), cconv, ck, cv, cffn, *weights)


def kernel(x_prompt, x_sample, cache_conv, cache_k, cache_v, cache_ffn_conv, g_attn_norm, w_in, conv_w, conv_b,
           conv_ln_g, conv_ln_b, sinks, g_out_conv, g_out_attn, w_out, g_ffn_norm, w_up, ffn_conv_w, ffn_conv_b,
           w_down, g_final):
    depth = w_in.shape[0]
    assert depth == 1, "single trunk layer"
    batch, seq, _ = x_prompt.shape
    n_streams, dec_seq, _ = x_sample.shape
    assert seq % PROMPT_TILE == 0 and PROMPT_TILE % QUERY_BLOCK == 0 and PROMPT_TILE % CONV_ROWS == 0
    assert n_streams % SAMPLE_STREAMS == 0 and dec_seq % (2 * SUBLANES) == 0
    assert CONV_WIDTH - 1 <= dec_seq <= KEY_BLOCK - WINDOW

    row = lambda a: a.reshape(1, -1)
    weights = (
        row(g_attn_norm[0]), w_in[0].astype(BF16), conv_w[0], row(conv_b[0]), row(conv_ln_g[0]),
        row(conv_ln_b[0]), sinks[0], row(g_out_conv[0]), row(g_out_attn[0]), w_out[0].astype(BF16),
        row(g_ffn_norm[0]), w_up[0].astype(BF16), ffn_conv_w[0], row(ffn_conv_b[0]), w_down[0].astype(BF16),
        row(g_final),
    )

    y_p, conv_p, k_p, v_p, ffn_p = _prompt_call(x_prompt, weights)

    cconv = jnp.pad(cache_conv[0], ((0, 0), (CONV_PAD - (CONV_WIDTH - 1), 0), (0, 0)))
    cffn = jnp.pad(cache_ffn_conv[0], ((0, 0), (FFN_PAD - (FFN_CONV_WIDTH - 1), 0), (0, 0)))
    ck = cache_k[0].reshape(n_streams, WINDOW, D_KV)
    cv = cache_v[0].reshape(n_streams, WINDOW, D_KV)
    y_s, conv_s, k_s, v_s, ffn_s = _sample_call(x_sample, cconv, ck, cv, cffn, weights)

    heads = lambda a: a.reshape(depth, a.shape[0], WINDOW, N_KV_HEADS, HEAD_DIM)
    return (y_p, y_s.reshape(n_streams, dec_seq, D_MODEL),
            conv_p[None], heads(k_p), heads(v_p), ffn_p[None],
            conv_s[None], heads(k_s), heads(v_s), ffn_s[None])
```

```python
import functools
import math

import jax
import jax.numpy as jnp
from jax import lax
from jax.experimental import pallas as pl
from jax.experimental.pallas import tpu as pltpu

D_MODEL = 1024
CHUNK = 64
D_CONV = 512
CONV_WIDTH = 31
N_Q_HEADS = 8
N_KV_HEADS = 2
HEAD_DIM = 64
D_ATTN = N_Q_HEADS * HEAD_DIM
D_KV = N_KV_HEADS * HEAD_DIM
WINDOW = 128
D_MIX = D_CONV + D_ATTN
D_IN = 2 * D_CONV + D_ATTN + 2 * D_KV
D_FF = 2816
FFN_CONV_WIDTH = 3
EPS = 1e-6
NEG_INF = -1e30
SCALE = HEAD_DIM ** -0.5

SUBLANES = 8
LANES = 128
MXU_COLS = 256
KEY_BLOCK = MXU_COLS
KV_PARTS = 2 * N_KV_HEADS
QUERY_BLOCK = KEY_BLOCK - WINDOW
CONV_PAD = 32
CONV_ROWS = 64
FFN_PAD = SUBLANES
FFN_COLS = MXU_COLS
N_FFN_PASSES = D_FF // FFN_COLS
PROMPT_TILE = 512
SAMPLE_STREAMS = 8
VMEM_LIMIT_BYTES = 58 * 1024 * 1024

BF16 = jnp.bfloat16
F32 = jnp.float32


def _dot(a, b):
    return jnp.dot(a, b, preferred_element_type=F32)


def _dot_t(a, b):
    return lax.dot_general(a, b, (((1,), (1,)), ((), ())), preferred_element_type=F32)


def _rms(x, g):
    return x * lax.rsqrt(jnp.mean(x * x, axis=-1, keepdims=True) + EPS) * g


def _gelu_tanh(x):
    c = math.sqrt(2.0 / math.pi)
    half = 0.5 * x
    return half + half * jnp.tanh(x * (c + (c * 0.044715) * (x * x)))


def _block_diag_parts(a):
    lo = lax.broadcasted_iota(jnp.int32, a.shape, 1) < HEAD_DIM
    swapped = pltpu.roll(a, HEAD_DIM, 1)
    zero = jnp.zeros_like(a)
    parts = (jnp.where(lo, a, zero), jnp.where(lo, zero, swapped),
             jnp.where(lo, swapped, zero), jnp.where(lo, zero, a))
    return [p.astype(BF16) for p in parts]


def _conv_mix_block(glu_view, r0, rows, shift_buf, cw_ref, cb_ref, lng_ref, lnb_ref, goc_ref):
    first = CONV_PAD - (CONV_WIDTH - 1)
    span = rows + CONV_PAD - SUBLANES
    for a in range(1, SUBLANES):
        shift_buf[a - 1, 0:span, :] = glu_view[r0 + a:r0 + a + span, :]
    acc = jnp.broadcast_to(cb_ref[...], (rows, D_CONV))
    for j in range(CONV_WIDTH):
        a, base = (first + j) % SUBLANES, (first + j) // SUBLANES * SUBLANES
        if a == 0:
            src = glu_view[r0 + base:r0 + base + rows, :]
        else:
            src = shift_buf[a - 1, base:base + rows, :]
        acc = acc + cw_ref[j:j + 1, :] * src
    mu = jnp.mean(acc, axis=-1, keepdims=True)
    xc = acc - mu
    y = xc * lax.rsqrt(jnp.mean(xc * xc, axis=-1, keepdims=True) + EPS) * lng_ref[...] + lnb_ref[...]
    cy = y * jax.nn.sigmoid(y)
    return _rms(cy, goc_ref[...])


def _attn_block(q_ref, qrow, rows, kget, vget, valid, sinks_ref, out_ref):
    for kv_head in range(N_KV_HEADS):
        pairs = (2 * kv_head, 2 * kv_head + 1)
        q = jnp.concatenate([q_ref[qrow:qrow + rows, LANES * p:LANES * (p + 1)] for p in pairs], axis=0)
        pv, inv = [], []
        for half in range(2):
            s = _dot_t(q, kget(2 * kv_head + half))
            e_parts, inv_half = [], []
            for i, p in enumerate(pairs):
                sp = jnp.where(valid, s[i * rows:(i + 1) * rows, :], NEG_INF)
                sink = sinks_ref[2 * p + half]
                m = jnp.maximum(jnp.max(sp, axis=-1, keepdims=True), sink)
                e = jnp.exp(sp - m)
                inv_half.append(1.0 / (jnp.sum(e, axis=-1, keepdims=True) + jnp.exp(sink - m)))
                e_parts.append(e.astype(BF16))
            pv.append(_dot(jnp.concatenate(e_parts, axis=0), vget(2 * kv_head + half)))
            inv.append(inv_half)
        for i, p in enumerate(pairs):
            rs = slice(i * rows, (i + 1) * rows)
            out_ref[qrow:qrow + rows, LANES * p:LANES * (p + 1)] = pv[0][rs] * inv[0][i] + pv[1][rs] * inv[1][i]


def _sample_attention(q_ref, kbd, vbd, s_buf, e_buf, sinks_ref, out_ref, streams, seq, n_keys):
    valid = lax.broadcasted_iota(jnp.int32, (seq, KEY_BLOCK), 1) < n_keys
    units = [(kv_head, half) for kv_head in range(N_KV_HEADS) for half in range(2)]
    for s in range(streams):
        row = s * seq
        for kv_head in range(N_KV_HEADS):
            q = jnp.concatenate([q_ref[row:row + seq, LANES * p:LANES * (p + 1)]
                                 for p in (2 * kv_head, 2 * kv_head + 1)], axis=0)
            for half in range(2):
                s_buf[s, 2 * kv_head + half] = _dot_t(q, kbd[2 * kv_head + half, s])
    for s in range(streams):
        for kv_head, half in units:
            scores = s_buf[s, 2 * kv_head + half]
            for i, p in enumerate((2 * kv_head, 2 * kv_head + 1)):
                sp = jnp.where(valid, scores[i * seq:(i + 1) * seq, :], NEG_INF)
                sink = sinks_ref[2 * p + half]
                m = jnp.maximum(jnp.max(sp, axis=-1, keepdims=True), sink)
                e = jnp.exp(sp - m)
                denom = jnp.sum(e, axis=-1, keepdims=True) + jnp.exp(sink - m)
                e_buf[s, 2 * kv_head + half, i * seq:(i + 1) * seq, :] = (e * (1.0 / denom)).astype(BF16)
    for s in range(streams):
        row = s * seq
        for kv_head in range(N_KV_HEADS):
            pv = [_dot(e_buf[s, 2 * kv_head + half], vbd[2 * kv_head + half, s]) for half in range(2)]
            for i, p in enumerate((2 * kv_head, 2 * kv_head + 1)):
                rs = slice(i * seq, (i + 1) * seq)
                out_ref[row:row + seq, LANES * p:LANES * (p + 1)] = pv[0][rs] + pv[1][rs]


def _ffn_up(c, h2, gate_buf, val_buf, w_up_ref):
    segments, seg_rows = gate_buf.shape[0], gate_buf.shape[1] - FFN_PAD
    cols = slice(c * FFN_COLS, (c + 1) * FFN_COLS)
    gate_buf[:, FFN_PAD:, cols] = _dot(h2, w_up_ref[:, cols]).reshape(segments, seg_rows, FFN_COLS)
    val_buf[c % 2] = _dot(h2, w_up_ref[:, D_FF + c * FFN_COLS:D_FF + (c + 1) * FFN_COLS])


def _ffn_act(c, gate_buf, lag_buf, val_buf, act_buf, fcw_ref, fcb_ref):
    segments, seg_rows = gate_buf.shape[0], gate_buf.shape[1] - FFN_PAD
    cols = slice(c * FFN_COLS, (c + 1) * FFN_COLS)
    gc = fcb_ref[:, cols] + fcw_ref[FFN_CONV_WIDTH - 1:FFN_CONV_WIDTH, cols] * gate_buf[:, FFN_PAD:, cols]
    for lag in range(1, FFN_CONV_WIDTH):
        lag_buf[c % 2, lag - 1] = gate_buf[:, FFN_PAD - lag:FFN_PAD - lag + seg_rows, cols]
        j = FFN_CONV_WIDTH - 1 - lag
        gc = gc + fcw_ref[j:j + 1, cols] * lag_buf[c % 2, lag - 1]
    act = _gelu_tanh(gc).reshape(segments * seg_rows, FFN_COLS) * val_buf[c % 2]
    act_buf[:, cols] = act.astype(BF16)


def _ffn_hidden(h2, gate_buf, lag_buf, val_buf, act_buf, w_up_ref, fcw_ref, fcb_ref):
    _ffn_up(0, h2, gate_buf, val_buf, w_up_ref)
    for c in range(N_FFN_PASSES):
        if c + 1 < N_FFN_PASSES:
            _ffn_up(c + 1, h2, gate_buf, val_buf, w_up_ref)
        _ffn_act(c, gate_buf, lag_buf, val_buf, act_buf, fcw_ref, fcb_ref)


def _prompt_kernel(x_ref, ga_ref, w_in_ref, cw_ref, cb_ref, lng_ref, lnb_ref, sinks_ref, goc_ref, goa_ref,
                   w_out_ref, gffn_ref, w_up_ref, fcw_ref, fcb_ref, w_down_ref, gfin_ref,
                   y_ref, convn_ref, kn_ref, vn_ref, ffnn_ref,
                   glu_buf, shift_buf, q_buf, kbd, vbd, ao_buf, mix_buf, x1_buf, gate_buf, lag_buf, val_buf, act_buf,
                   *, tiles_per_stream):
    i = pl.program_id(0)
    tile = x_ref.shape[1]
    mixer_starts_stream = i % tiles_per_stream == 0
    ffn_starts_stream = (i + tiles_per_stream - 1) % tiles_per_stream == 0

    @pl.when(i == 0)
    def _():
        x1_buf[...] = jnp.zeros(x1_buf.shape, F32)

    @pl.when(mixer_starts_stream)
    def _():
        glu_buf[0:CONV_PAD, :] = jnp.zeros((CONV_PAD, D_CONV), F32)
        for buf in (kbd, vbd):
            buf[:, 0:WINDOW, :] = jnp.zeros((KV_PARTS, WINDOW, LANES), BF16)

    @pl.when(jnp.logical_or(i == 0, ffn_starts_stream))
    def _():
        gate_buf[:, 0:FFN_PAD, :] = jnp.zeros((1, FFN_PAD, D_FF), F32)

    hb = _rms(x_ref[0], ga_ref[...]).astype(BF16)
    ab = _dot(hb, w_in_ref[:, 0:2 * D_CONV])
    glu_buf[CONV_PAD:, :] = ab[:, 0:D_CONV] * jax.nn.sigmoid(ab[:, D_CONV:])

    q_buf[...] = (_dot(hb, w_in_ref[:, 2 * D_CONV:2 * D_CONV + D_ATTN]) * SCALE).astype(BF16)
    kv = _dot(hb, w_in_ref[:, 2 * D_CONV + D_ATTN:])
    k = kv[:, 0:D_KV]
    v = kv[:, D_KV:]
    for part_index, part in enumerate(_block_diag_parts(k)):
        kbd[part_index, WINDOW:, :] = part
    for part_index, part in enumerate(_block_diag_parts(v)):
        vbd[part_index, WINDOW:, :] = part

    kn_ref[0] = k[tile - WINDOW:, :]
    vn_ref[0] = v[tile - WINDOW:, :]
    convn_ref[0] = glu_buf[CONV_PAD + tile - (CONV_WIDTH - 1):, :]

    def conv_task(r):
        r0 = r * CONV_ROWS
        n = _conv_mix_block(glu_buf, r0, CONV_ROWS, shift_buf.at[r % 2], cw_ref, cb_ref, lng_ref, lnb_ref, goc_ref)
        mix_buf[r0:r0 + CONV_ROWS, 0:D_CONV] = n.astype(BF16)

    chunk_shift = CHUNK.bit_length() - 1
    q_chunk = lax.broadcasted_iota(jnp.int32, (QUERY_BLOCK, KEY_BLOCK), 0) >> chunk_shift
    k_chunk = lax.broadcasted_iota(jnp.int32, (QUERY_BLOCK, KEY_BLOCK), 1) >> chunk_shift
    band = (k_chunk >= q_chunk) & (k_chunk <= q_chunk + WINDOW // CHUNK)

    def attn_task(b):
        r0 = b * QUERY_BLOCK
        valid = band & (k_chunk >= jnp.where(mixer_starts_stream, WINDOW // CHUNK, 0)) if b == 0 else band
        _attn_block(q_buf, r0, QUERY_BLOCK,
                    lambda i: kbd[i, r0:r0 + KEY_BLOCK, :],
                    lambda i: vbd[i, r0:r0 + KEY_BLOCK, :],
                    valid, sinks_ref, ao_buf)

    x1_prev = x1_buf[...]
    h2 = _rms(x1_prev, gffn_ref[...]).astype(BF16)
    _ffn_hidden(h2, gate_buf, lag_buf, val_buf, act_buf, w_up_ref, fcw_ref, fcb_ref)
    ffnn_ref[...] = gate_buf[:, FFN_PAD + tile - (FFN_CONV_WIDTH - 1):, :]
    gate_buf[:, 0:FFN_PAD, :] = gate_buf[:, tile:, :]
    y_ref[0] = _rms(x1_prev + _dot(act_buf[...], w_down_ref[...]), gfin_ref[...])

    for r in range(tile // CONV_ROWS):
        conv_task(r)
    for b in range(tile // QUERY_BLOCK):
        attn_task(b)
    glu_buf[0:CONV_PAD, :] = glu_buf[tile:, :]
    for buf in (kbd, vbd):
        buf[:, 0:WINDOW, :] = buf[:, tile:, :]
    mix_buf[:, D_CONV:] = _rms(ao_buf[...], goa_ref[...]).astype(BF16)

    x1_buf[...] = x_ref[0] + _dot(mix_buf[...], w_out_ref[...])


def _sample_kernel(x_ref, cconv_ref, ck_ref, cv_ref, cffn_ref,
                   ga_ref, w_in_ref, cw_ref, cb_ref, lng_ref, lnb_ref, sinks_ref, goc_ref, goa_ref,
                   w_out_ref, gffn_ref, w_up_ref, fcw_ref, fcb_ref, w_down_ref, gfin_ref,
                   y_ref, convn_ref, kn_ref, vn_ref, ffnn_ref,
                   glu_buf, shift_buf, q_buf, kbd, vbd, s_buf, e_buf, ao_buf, mix_buf, x1_buf, gate_buf, lag_buf,
                   val_buf, act_buf):
    streams, seq = glu_buf.shape[0], glu_buf.shape[1] - CONV_PAD
    n_keys = WINDOW + seq

    hb = _rms(x_ref[...], ga_ref[...]).astype(BF16)
    ab = _dot(hb, w_in_ref[:, 0:2 * D_CONV])
    glu = ab[:, 0:D_CONV] * jax.nn.sigmoid(ab[:, D_CONV:])
    glu_buf[:, 0:CONV_PAD, :] = cconv_ref[...]
    glu_buf[:, CONV_PAD:, :] = glu.reshape(streams, seq, D_CONV)
    convn_ref[...] = glu_buf[:, CONV_PAD + seq - (CONV_WIDTH - 1):, :]
    q_buf[...] = (_dot(hb, w_in_ref[:, 2 * D_CONV:2 * D_CONV + D_ATTN]) * SCALE).astype(BF16)
    kv = _dot(hb, w_in_ref[:, 2 * D_CONV + D_ATTN:])
    for cache_ref, new_ref, buf, lanes in ((ck_ref, kn_ref, kbd, slice(0, D_KV)),
                                           (cv_ref, vn_ref, vbd, slice(D_KV, 2 * D_KV))):
        hist = cache_ref[...]
        new = kv[:, lanes].reshape(streams, seq, D_KV)
        new_ref[:, 0:WINDOW - seq, :] = hist[:, seq:, :]
        new_ref[:, WINDOW - seq:, :] = new
        for i, part in enumerate(_block_diag_parts(hist.reshape(streams * WINDOW, D_KV))):
            buf[i, :, 0:WINDOW, :] = part.reshape(streams, WINDOW, LANES)
        for i, part in enumerate(_block_diag_parts(kv[:, lanes])):
            buf[i, :, WINDOW:n_keys, :] = part.reshape(streams, seq, LANES)
        buf[:, :, n_keys:, :] = jnp.zeros((KV_PARTS, streams, KEY_BLOCK - n_keys, LANES), BF16)

    _sample_attention(q_buf, kbd, vbd, s_buf, e_buf, sinks_ref, ao_buf, streams, seq, n_keys)
    for s in range(streams):
        row = s * seq
        n = _conv_mix_block(glu_buf.at[s], 0, seq, shift_buf.at[s % 2], cw_ref, cb_ref, lng_ref, lnb_ref, goc_ref)
        mix_buf[row:row + seq, 0:D_CONV] = n.astype(BF16)
    mix_buf[:, D_CONV:] = _rms(ao_buf[...], goa_ref[...]).astype(BF16)

    x1_buf[...] = x_ref[...] + _dot(mix_buf[...], w_out_ref[...])
    h2 = _rms(x1_buf[...], gffn_ref[...]).astype(BF16)

    gate_buf[:, 0:FFN_PAD, :] = cffn_ref[...]
    _ffn_hidden(h2, gate_buf, lag_buf, val_buf, act_buf, w_up_ref, fcw_ref, fcb_ref)
    ffnn_ref[...] = gate_buf[:, FFN_PAD + seq - (FFN_CONV_WIDTH - 1):, :]
    y_ref[...] = _rms(x1_buf[...] + _dot(act_buf[...], w_down_ref[...]), gfin_ref[...])


def _resident(shape):
    zeros = (0,) * len(shape)
    return pl.BlockSpec(shape, lambda *_: zeros, pipeline_mode=pl.Buffered(1))


def _weight_specs():
    row = lambda n: _resident((1, n))
    return [
        row(D_MODEL),
        _resident((D_MODEL, D_IN)),
        _resident((CONV_WIDTH, D_CONV)),
        row(D_CONV), row(D_CONV), row(D_CONV),
        pl.BlockSpec(memory_space=pltpu.SMEM),
        row(D_CONV), row(D_ATTN),
        _resident((D_MIX, D_MODEL)),
        row(D_MODEL),
        _resident((D_MODEL, 2 * D_FF)),
        _resident((FFN_CONV_WIDTH, D_FF)),
        row(D_FF),
        _resident((D_FF, D_MODEL)),
        row(D_MODEL),
    ]


def _shift_scratch(rows):
    return pltpu.VMEM((2, SUBLANES - 1, rows + CONV_PAD - SUBLANES, D_CONV), F32)


def _prompt_call(x, weights):
    batch, seq, _ = x.shape
    tile = PROMPT_TILE
    per_stream = seq // tile
    n_tiles = batch * per_stream
    mixer_tile = lambda i: jnp.minimum(i, n_tiles - 1)
    ffn_tile = lambda i: jnp.maximum(i - 1, 0)
    mixer_out = lambda shape: pl.BlockSpec((1,) + shape, lambda i: (mixer_tile(i) // per_stream, 0, 0))
    return pl.pallas_call(
        functools.partial(_prompt_kernel, tiles_per_stream=per_stream),
        grid=(n_tiles + 1,),
        in_specs=[pl.BlockSpec((1, tile, D_MODEL),
                               lambda i: (mixer_tile(i) // per_stream, mixer_tile(i) % per_stream, 0))]
        + _weight_specs(),
        out_specs=[
            pl.BlockSpec((1, tile, D_MODEL), lambda i: (ffn_tile(i) // per_stream, ffn_tile(i) % per_stream, 0)),
            mixer_out((CONV_WIDTH - 1, D_CONV)),
            mixer_out((WINDOW, D_KV)),
            mixer_out((WINDOW, D_KV)),
            pl.BlockSpec((1, FFN_CONV_WIDTH - 1, D_FF), lambda i: (ffn_tile(i) // per_stream, 0, 0)),
        ],
        out_shape=[
            jax.ShapeDtypeStruct((batch, seq, D_MODEL), F32),
            jax.ShapeDtypeStruct((batch, CONV_WIDTH - 1, D_CONV), F32),
            jax.ShapeDtypeStruct((batch, WINDOW, D_KV), F32),
            jax.ShapeDtypeStruct((batch, WINDOW, D_KV), F32),
            jax.ShapeDtypeStruct((batch, FFN_CONV_WIDTH - 1, D_FF), F32),
        ],
        scratch_shapes=[
            pltpu.VMEM((CONV_PAD + tile, D_CONV), F32),
            _shift_scratch(CONV_ROWS),
            pltpu.VMEM((tile, D_ATTN), BF16),
            pltpu.VMEM((KV_PARTS, WINDOW + tile, LANES), BF16),
            pltpu.VMEM((KV_PARTS, WINDOW + tile, LANES), BF16),
            pltpu.VMEM((tile, D_ATTN), F32),
            pltpu.VMEM((tile, D_MIX), BF16),
            pltpu.VMEM((tile, D_MODEL), F32),
            pltpu.VMEM((1, FFN_PAD + tile, D_FF), F32),
            pltpu.VMEM((2, FFN_CONV_WIDTH - 1, 1, tile, FFN_COLS), F32),
            pltpu.VMEM((2, tile, FFN_COLS), F32),
            pltpu.VMEM((tile, D_FF), BF16),
        ],
        compiler_params=pltpu.CompilerParams(
            dimension_semantics=("arbitrary",),
            vmem_limit_bytes=VMEM_LIMIT_BYTES),
        name="prompt_layer",
    )(x, *weights)


def _sample_call(x, cconv, ck, cv, cffn, weights):
    n_streams, seq, _ = x.shape
    sb = SAMPLE_STREAMS
    rows = sb * seq
    per_stream = lambda shape: pl.BlockSpec((sb,) + shape, lambda i: (i, 0, 0))
    return pl.pallas_call(
        _sample_kernel,
        grid=(n_streams // sb,),
        in_specs=[
            pl.BlockSpec((rows, D_MODEL), lambda i: (i, 0)),
            per_stream((CONV_PAD, D_CONV)),
            per_stream((WINDOW, D_KV)),
            per_stream((WINDOW, D_KV)),
            per_stream((FFN_PAD, D_FF)),
        ] + _weight_specs(),
        out_specs=[
            pl.BlockSpec((rows, D_MODEL), lambda i: (i, 0)),
            per_stream((CONV_WIDTH - 1, D_CONV)),
            per_stream((WINDOW, D_KV)),
            per_stream((WINDOW, D_KV)),
            per_stream((FFN_CONV_WIDTH - 1, D_FF)),
        ],
        out_shape=[
            jax.ShapeDtypeStruct((n_streams * seq, D_MODEL), F32),
            jax.ShapeDtypeStruct((n_streams, CONV_WIDTH - 1, D_CONV), F32),
            jax.ShapeDtypeStruct((n_streams, WINDOW, D_KV), F32),
            jax.ShapeDtypeStruct((n_streams, WINDOW, D_KV), F32),
            jax.ShapeDtypeStruct((n_streams, FFN_CONV_WIDTH - 1, D_FF), F32),
        ],
        scratch_shapes=[
            pltpu.VMEM((sb, CONV_PAD + seq, D_CONV), F32),
            _shift_scratch(seq),
            pltpu.VMEM((rows, D_ATTN), BF16),
            pltpu.VMEM((KV_PARTS, sb, KEY_BLOCK, LANES), BF16),
            pltpu.VMEM((KV_PARTS, sb, KEY_BLOCK, LANES), BF16),
            pltpu.VMEM((sb, KV_PARTS, 2 * seq, KEY_BLOCK), F32),
            pltpu.VMEM((sb, KV_PARTS, 2 * seq, KEY_BLOCK), BF16),
            pltpu.VMEM((rows, D_ATTN), F32),
            pltpu.VMEM((rows, D_MIX), BF16),
            pltpu.VMEM((rows, D_MODEL), F32),
            pltpu.VMEM((sb, FFN_PAD + seq, D_FF), F32),
            pltpu.VMEM((2, FFN_CONV_WIDTH - 1, sb, seq, FFN_COLS), F32),
            pltpu.VMEM((2, rows, FFN_COLS), F32),
            pltpu.VMEM((rows, D_FF), BF16),
        ],
        compiler_params=pltpu.CompilerParams(
            dimension_semantics=("arbitrary",),
            vmem_limit_bytes=VMEM_LIMIT_BYTES),
        name="sample_layer",
    )(x.reshape(n_streams * seq, D_MODEL), cconv, ck, cv, cffn, *weights)


def kernel(x_prompt, x_sample, cache_conv, cache_k, cache_v, cache_ffn_conv, g_attn_norm, w_in, conv_w, conv_b,
           conv_ln_g, conv_ln_b, sinks, g_out_conv, g_out_attn, w_out, g_ffn_norm, w_up, ffn_conv_w, ffn_conv_b,
           w_down, g_final):
    depth = w_in.shape[0]
    assert depth == 1, "single trunk layer"
    batch, seq, _ = x_prompt.shape
    n_streams, dec_seq, _ = x_sample.shape
    assert seq % PROMPT_TILE == 0 and PROMPT_TILE % QUERY_BLOCK == 0 and PROMPT_TILE % CONV_ROWS == 0
    assert n_streams % SAMPLE_STREAMS == 0 and dec_seq % (2 * SUBLANES) == 0
    assert CONV_WIDTH - 1 <= dec_seq <= KEY_BLOCK - WINDOW

    row = lambda a: a.reshape(1, -1)
    weights = (
        row(g_attn_norm[0]), w_in[0].astype(BF16), conv_w[0], row(conv_b[0]), row(conv_ln_g[0]),
        row(conv_ln_b[0]), sinks[0], row(g_out_conv[0]), row(g_out_attn[0]), w_out[0].astype(BF16),
        row(g_ffn_norm[0]), w_up[0].astype(BF16), ffn_conv_w[0], row(ffn_conv_b[0]), w_down[0].astype(BF16),
        row(g_final),
    )

    y_p, conv_p, k_p, v_p, ffn_p = _prompt_call(x_prompt, weights)

    cconv = jnp.pad(cache_conv[0], ((0, 0), (CONV_PAD - (CONV_WIDTH - 1), 0), (0, 0)))
    cffn = jnp.pad(cache_ffn_conv[0], ((0, 0), (FFN_PAD - (FFN_CONV_WIDTH - 1), 0), (0, 0)))
    ck = cache_k[0].reshape(n_streams, WINDOW, D_KV)
    cv = cache_v[0].reshape(n_streams, WINDOW, D_KV)
    y_s, conv_s, k_s, v_s, ffn_s = _sample_call(x_sample, cconv, ck, cv, cffn, weights)

    heads = lambda a: a.reshape(depth, a.shape[0], WINDOW, N_KV_HEADS, HEAD_DIM)
    return (y_p, y_s.reshape(n_streams, dec_seq, D_MODEL),
            conv_p[None], heads(k_p), heads(v_p), ffn_p[None],
            conv_s[None], heads(k_s), heads(v_s), ffn_s[None])
```

```python
import functools
import math

import jax
import jax.numpy as jnp
from jax import lax
from jax.experimental import pallas as pl
from jax.experimental.pallas import tpu as pltpu

D_MODEL = 1024
CHUNK = 64
D_CONV = 512
CONV_WIDTH = 31
N_Q_HEADS = 8
N_KV_HEADS = 2
HEAD_DIM = 64
D_ATTN = N_Q_HEADS * HEAD_DIM
D_KV = N_KV_HEADS * HEAD_DIM
WINDOW = 128
D_MIX = D_CONV + D_ATTN
D_IN = 2 * D_CONV + D_ATTN + 2 * D_KV
D_FF = 2816
FFN_CONV_WIDTH = 3
EPS = 1e-6
NEG_INF = -1e30
SCALE = HEAD_DIM ** -0.5

SUBLANES = 8
LANES = 128
MXU_COLS = 256
KEY_BLOCK = MXU_COLS
KV_PARTS = 2 * N_KV_HEADS
QUERY_BLOCK = KEY_BLOCK - WINDOW
CONV_PAD = 32
CONV_ROWS = 64
FFN_PAD = SUBLANES
FFN_COLS = MXU_COLS
N_FFN_PASSES = D_FF // FFN_COLS
PROMPT_TILE = 512
SAMPLE_STREAMS = 8
VMEM_LIMIT_BYTES = 58 * 1024 * 1024

BF16 = jnp.bfloat16
F32 = jnp.float32


def _dot(a, b):
    return jnp.dot(a, b, preferred_element_type=F32)


def _dot_t(a, b):
    return lax.dot_general(a, b, (((1,), (1,)), ((), ())), preferred_element_type=F32)


def _rms(x, g):
    return x * lax.rsqrt(jnp.mean(x * x, axis=-1, keepdims=True) + EPS) * g


def _gelu_tanh(x):
    c = math.sqrt(2.0 / math.pi)
    half = 0.5 * x
    return half + half * jnp.tanh(x * (c + (c * 0.044715) * (x * x)))


def _block_diag_parts(a):
    lo = lax.broadcasted_iota(jnp.int32, a.shape, 1) < HEAD_DIM
    swapped = pltpu.roll(a, HEAD_DIM, 1)
    zero = jnp.zeros_like(a)
    parts = (jnp.where(lo, a, zero), jnp.where(lo, zero, swapped),
             jnp.where(lo, swapped, zero), jnp.where(lo, zero, a))
    return [p.astype(BF16) for p in parts]


def _conv_mix_block(glu_view, r0, rows, shift_buf, cw_ref, cb_ref, lng_ref, lnb_ref, goc_ref):
    first = CONV_PAD - (CONV_WIDTH - 1)
    span = rows + CONV_PAD - SUBLANES
    for a in range(1, SUBLANES):
        shift_buf[a - 1, 0:span, :] = glu_view[r0 + a:r0 + a + span, :]
    acc = jnp.broadcast_to(cb_ref[...], (rows, D_CONV))
    for j in range(CONV_WIDTH):
        a, base = (first + j) % SUBLANES, (first + j) // SUBLANES * SUBLANES
        if a == 0:
            src = glu_view[r0 + base:r0 + base + rows, :]
        else:
            src = shift_buf[a - 1, base:base + rows, :]
        acc = acc + cw_ref[j:j + 1, :] * src
    mu = jnp.mean(acc, axis=-1, keepdims=True)
    xc = acc - mu
    y = xc * lax.rsqrt(jnp.mean(xc * xc, axis=-1, keepdims=True) + EPS) * lng_ref[...] + lnb_ref[...]
    cy = y * jax.nn.sigmoid(y)
    return _rms(cy, goc_ref[...])


def _attn_block(q_ref, qrow, rows, kget, vget, valid, sinks_ref, out_ref):
    scores = {}
    for kv_head in range(N_KV_HEADS):
        pairs = (2 * kv_head, 2 * kv_head + 1)
        q = jnp.concatenate([q_ref[qrow:qrow + rows, LANES * p:LANES * (p + 1)] for p in pairs], axis=0)
        for half in range(2):
            scores[kv_head, half] = _dot_t(q, kget(2 * kv_head + half))
    for kv_head in range(N_KV_HEADS):
        pairs = (2 * kv_head, 2 * kv_head + 1)
        pv, inv = [], []
        for half in range(2):
            s = scores[kv_head, half]
            e_parts, inv_half = [], []
            for i, p in enumerate(pairs):
                sp = jnp.where(valid, s[i * rows:(i + 1) * rows, :], NEG_INF)
                sink = sinks_ref[2 * p + half]
                m = jnp.maximum(jnp.max(sp, axis=-1, keepdims=True), sink)
                e = jnp.exp(sp - m)
                inv_half.append(1.0 / (jnp.sum(e, axis=-1, keepdims=True) + jnp.exp(sink - m)))
                e_parts.append(e.astype(BF16))
            pv.append(_dot(jnp.concatenate(e_parts, axis=0), vget(2 * kv_head + half)))
            inv.append(inv_half)
        for i, p in enumerate(pairs):
            rs = slice(i * rows, (i + 1) * rows)
            out_ref[qrow:qrow + rows, LANES * p:LANES * (p + 1)] = pv[0][rs] * inv[0][i] + pv[1][rs] * inv[1][i]


def _sample_attention(q_ref, kbd, vbd, s_buf, e_buf, sinks_ref, out_ref, streams, seq, n_keys):
    valid = lax.broadcasted_iota(jnp.int32, (seq, KEY_BLOCK), 1) < n_keys
    units = [(kv_head, half) for kv_head in range(N_KV_HEADS) for half in range(2)]
    for s in range(streams):
        row = s * seq
        for kv_head in range(N_KV_HEADS):
            q = jnp.concatenate([q_ref[row:row + seq, LANES * p:LANES * (p + 1)]
                                 for p in (2 * kv_head, 2 * kv_head + 1)], axis=0)
            for half in range(2):
                s_buf[s, 2 * kv_head + half] = _dot_t(q, kbd[2 * kv_head + half, s])
    for s in range(streams):
        for kv_head, half in units:
            scores = s_buf[s, 2 * kv_head + half]
            for i, p in enumerate((2 * kv_head, 2 * kv_head + 1)):
                sp = jnp.where(valid, scores[i * seq:(i + 1) * seq, :], NEG_INF)
                sink = sinks_ref[2 * p + half]
                m = jnp.maximum(jnp.max(sp, axis=-1, keepdims=True), sink)
                e = jnp.exp(sp - m)
                denom = jnp.sum(e, axis=-1, keepdims=True) + jnp.exp(sink - m)
                e_buf[s, 2 * kv_head + half, i * seq:(i + 1) * seq, :] = (e * (1.0 / denom)).astype(BF16)
    for s in range(streams):
        row = s * seq
        for kv_head in range(N_KV_HEADS):
            pv = [_dot(e_buf[s, 2 * kv_head + half], vbd[2 * kv_head + half, s]) for half in range(2)]
            for i, p in enumerate((2 * kv_head, 2 * kv_head + 1)):
                rs = slice(i * seq, (i + 1) * seq)
                out_ref[row:row + seq, LANES * p:LANES * (p + 1)] = pv[0][rs] + pv[1][rs]


def _ffn_up(c, h2, gate_buf, val_buf, w_up_ref):
    segments, seg_rows = gate_buf.shape[0], gate_buf.shape[1] - FFN_PAD
    cols = slice(c * FFN_COLS, (c + 1) * FFN_COLS)
    gate_buf[:, FFN_PAD:, cols] = _dot(h2, w_up_ref[:, cols]).reshape(segments, seg_rows, FFN_COLS)
    val_buf[c % 2] = _dot(h2, w_up_ref[:, D_FF + c * FFN_COLS:D_FF + (c + 1) * FFN_COLS])


def _ffn_act(c, gate_buf, lag_buf, val_buf, act_buf, fcw_ref, fcb_ref):
    segments, seg_rows = gate_buf.shape[0], gate_buf.shape[1] - FFN_PAD
    cols = slice(c * FFN_COLS, (c + 1) * FFN_COLS)
    gc = fcb_ref[:, cols] + fcw_ref[FFN_CONV_WIDTH - 1:FFN_CONV_WIDTH, cols] * gate_buf[:, FFN_PAD:, cols]
    for lag in range(1, FFN_CONV_WIDTH):
        lag_buf[c % 2, lag - 1] = gate_buf[:, FFN_PAD - lag:FFN_PAD - lag + seg_rows, cols]
        j = FFN_CONV_WIDTH - 1 - lag
        gc = gc + fcw_ref[j:j + 1, cols] * lag_buf[c % 2, lag - 1]
    act = _gelu_tanh(gc).reshape(segments * seg_rows, FFN_COLS) * val_buf[c % 2]
    act_buf[:, cols] = act.astype(BF16)


def _ffn_hidden(h2, gate_buf, lag_buf, val_buf, act_buf, w_up_ref, fcw_ref, fcb_ref):
    _ffn_up(0, h2, gate_buf, val_buf, w_up_ref)
    for c in range(N_FFN_PASSES):
        if c + 1 < N_FFN_PASSES:
            _ffn_up(c + 1, h2, gate_buf, val_buf, w_up_ref)
        _ffn_act(c, gate_buf, lag_buf, val_buf, act_buf, fcw_ref, fcb_ref)


def _prompt_kernel(x_ref, ga_ref, w_in_ref, cw_ref, cb_ref, lng_ref, lnb_ref, sinks_ref, goc_ref, goa_ref,
                   w_out_ref, gffn_ref, w_up_ref, fcw_ref, fcb_ref, w_down_ref, gfin_ref,
                   y_ref, convn_ref, kn_ref, vn_ref, ffnn_ref,
                   glu_buf, shift_buf, q_buf, kbd, vbd, ao_buf, mix_buf, x1_buf, gate_buf, lag_buf, val_buf, act_buf,
                   *, tiles_per_stream):
    i = pl.program_id(0)
    tile = x_ref.shape[1]
    mixer_starts_stream = i % tiles_per_stream == 0
    ffn_starts_stream = (i + tiles_per_stream - 1) % tiles_per_stream == 0

    @pl.when(i == 0)
    def _():
        x1_buf[...] = jnp.zeros(x1_buf.shape, F32)

    @pl.when(mixer_starts_stream)
    def _():
        glu_buf[0:CONV_PAD, :] = jnp.zeros((CONV_PAD, D_CONV), F32)
        for buf in (kbd, vbd):
            buf[:, 0:WINDOW, :] = jnp.zeros((KV_PARTS, WINDOW, LANES), BF16)

    @pl.when(jnp.logical_or(i == 0, ffn_starts_stream))
    def _():
        gate_buf[:, 0:FFN_PAD, :] = jnp.zeros((1, FFN_PAD, D_FF), F32)

    hb = _rms(x_ref[0], ga_ref[...]).astype(BF16)
    ab = _dot(hb, w_in_ref[:, 0:2 * D_CONV])
    glu_buf[CONV_PAD:, :] = ab[:, 0:D_CONV] * jax.nn.sigmoid(ab[:, D_CONV:])

    q_buf[...] = (_dot(hb, w_in_ref[:, 2 * D_CONV:2 * D_CONV + D_ATTN]) * SCALE).astype(BF16)
    kv = _dot(hb, w_in_ref[:, 2 * D_CONV + D_ATTN:])
    k = kv[:, 0:D_KV]
    v = kv[:, D_KV:]
    for part_index, part in enumerate(_block_diag_parts(k)):
        kbd[part_index, WINDOW:, :] = part
    for part_index, part in enumerate(_block_diag_parts(v)):
        vbd[part_index, WINDOW:, :] = part

    kn_ref[0] = k[tile - WINDOW:, :]
    vn_ref[0] = v[tile - WINDOW:, :]
    convn_ref[0] = glu_buf[CONV_PAD + tile - (CONV_WIDTH - 1):, :]

    def conv_task(r):
        r0 = r * CONV_ROWS
        n = _conv_mix_block(glu_buf, r0, CONV_ROWS, shift_buf.at[r % 2], cw_ref, cb_ref, lng_ref, lnb_ref, goc_ref)
        mix_buf[r0:r0 + CONV_ROWS, 0:D_CONV] = n.astype(BF16)

    chunk_shift = CHUNK.bit_length() - 1
    q_chunk = lax.broadcasted_iota(jnp.int32, (QUERY_BLOCK, KEY_BLOCK), 0) >> chunk_shift
    k_chunk = lax.broadcasted_iota(jnp.int32, (QUERY_BLOCK, KEY_BLOCK), 1) >> chunk_shift
    band = (k_chunk >= q_chunk) & (k_chunk <= q_chunk + WINDOW // CHUNK)

    def attn_task(b):
        r0 = b * QUERY_BLOCK
        valid = band & (k_chunk >= jnp.where(mixer_starts_stream, WINDOW // CHUNK, 0)) if b == 0 else band
        _attn_block(q_buf, r0, QUERY_BLOCK,
                    lambda i: kbd[i, r0:r0 + KEY_BLOCK, :],
                    lambda i: vbd[i, r0:r0 + KEY_BLOCK, :],
                    valid, sinks_ref, ao_buf)

    x1_prev = x1_buf[...]
    h2 = _rms(x1_prev, gffn_ref[...]).astype(BF16)
    _ffn_hidden(h2, gate_buf, lag_buf, val_buf, act_buf, w_up_ref, fcw_ref, fcb_ref)
    ffnn_ref[...] = gate_buf[:, FFN_PAD + tile - (FFN_CONV_WIDTH - 1):, :]
    gate_buf[:, 0:FFN_PAD, :] = gate_buf[:, tile:, :]
    y_ref[0] = _rms(x1_prev + _dot(act_buf[...], w_down_ref[...]), gfin_ref[...])

    for r in range(tile // CONV_ROWS):
        conv_task(r)
    for b in range(tile // QUERY_BLOCK):
        attn_task(b)
    glu_buf[0:CONV_PAD, :] = glu_buf[tile:, :]
    for buf in (kbd, vbd):
        buf[:, 0:WINDOW, :] = buf[:, tile:, :]
    mix_buf[:, D_CONV:] = _rms(ao_buf[...], goa_ref[...]).astype(BF16)

    x1_buf[...] = x_ref[0] + _dot(mix_buf[...], w_out_ref[...])


def _sample_kernel(x_ref, cconv_ref, ck_ref, cv_ref, cffn_ref,
                   ga_ref, w_in_ref, cw_ref, cb_ref, lng_ref, lnb_ref, sinks_ref, goc_ref, goa_ref,
                   w_out_ref, gffn_ref, w_up_ref, fcw_ref, fcb_ref, w_down_ref, gfin_ref,
                   y_ref, convn_ref, kn_ref, vn_ref, ffnn_ref,
                   glu_buf, shift_buf, q_buf, kbd, vbd, s_buf, e_buf, ao_buf, mix_buf, x1_buf, gate_buf, lag_buf,
                   val_buf, act_buf):
    streams, seq = glu_buf.shape[0], glu_buf.shape[1] - CONV_PAD
    n_keys = WINDOW + seq

    hb = _rms(x_ref[...], ga_ref[...]).astype(BF16)
    ab = _dot(hb, w_in_ref[:, 0:2 * D_CONV])
    glu = ab[:, 0:D_CONV] * jax.nn.sigmoid(ab[:, D_CONV:])
    glu_buf[:, 0:CONV_PAD, :] = cconv_ref[...]
    glu_buf[:, CONV_PAD:, :] = glu.reshape(streams, seq, D_CONV)
    convn_ref[...] = glu_buf[:, CONV_PAD + seq - (CONV_WIDTH - 1):, :]
    q_buf[...] = (_dot(hb, w_in_ref[:, 2 * D_CONV:2 * D_CONV + D_ATTN]) * SCALE).astype(BF16)
    kv = _dot(hb, w_in_ref[:, 2 * D_CONV + D_ATTN:])
    for cache_ref, new_ref, buf, lanes in ((ck_ref, kn_ref, kbd, slice(0, D_KV)),
                                           (cv_ref, vn_ref, vbd, slice(D_KV, 2 * D_KV))):
        hist = cache_ref[...]
        new = kv[:, lanes].reshape(streams, seq, D_KV)
        new_ref[:, 0:WINDOW - seq, :] = hist[:, seq:, :]
        new_ref[:, WINDOW - seq:, :] = new
        for i, part in enumerate(_block_diag_parts(hist.reshape(streams * WINDOW, D_KV))):
            buf[i, :, 0:WINDOW, :] = part.reshape(streams, WINDOW, LANES)
        for i, part in enumerate(_block_diag_parts(kv[:, lanes])):
            buf[i, :, WINDOW:n_keys, :] = part.reshape(streams, seq, LANES)
        buf[:, :, n_keys:, :] = jnp.zeros((KV_PARTS, streams, KEY_BLOCK - n_keys, LANES), BF16)

    _sample_attention(q_buf, kbd, vbd, s_buf, e_buf, sinks_ref, ao_buf, streams, seq, n_keys)
    for s in range(streams):
        row = s * seq
        n = _conv_mix_block(glu_buf.at[s], 0, seq, shift_buf.at[s % 2], cw_ref, cb_ref, lng_ref, lnb_ref, goc_ref)
        mix_buf[row:row + seq, 0:D_CONV] = n.astype(BF16)
    mix_buf[:, D_CONV:] = _rms(ao_buf[...], goa_ref[...]).astype(BF16)

    x1_buf[...] = x_ref[...] + _dot(mix_buf[...], w_out_ref[...])
    h2 = _rms(x1_buf[...], gffn_ref[...]).astype(BF16)

    gate_buf[:, 0:FFN_PAD, :] = cffn_ref[...]
    _ffn_hidden(h2, gate_buf, lag_buf, val_buf, act_buf, w_up_ref, fcw_ref, fcb_ref)
    ffnn_ref[...] = gate_buf[:, FFN_PAD + seq - (FFN_CONV_WIDTH - 1):, :]
    y_ref[...] = _rms(x1_buf[...] + _dot(act_buf[...], w_down_ref[...]), gfin_ref[...])


def _resident(shape):
    zeros = (0,) * len(shape)
    return pl.BlockSpec(shape, lambda *_: zeros, pipeline_mode=pl.Buffered(1))


def _weight_specs():
    row = lambda n: _resident((1, n))
    return [
        row(D_MODEL),
        _resident((D_MODEL, D_IN)),
        _resident((CONV_WIDTH, D_CONV)),
        row(D_CONV), row(D_CONV), row(D_CONV),
        pl.BlockSpec(memory_space=pltpu.SMEM),
        row(D_CONV), row(D_ATTN),
        _resident((D_MIX, D_MODEL)),
        row(D_MODEL),
        _resident((D_MODEL, 2 * D_FF)),
        _resident((FFN_CONV_WIDTH, D_FF)),
        row(D_FF),
        _resident((D_FF, D_MODEL)),
        row(D_MODEL),
    ]


def _shift_scratch(rows):
    return pltpu.VMEM((2, SUBLANES - 1, rows + CONV_PAD - SUBLANES, D_CONV), F32)


def _prompt_call(x, weights):
    batch, seq, _ = x.shape
    tile = PROMPT_TILE
    per_stream = seq // tile
    n_tiles = batch * per_stream
    mixer_tile = lambda i: jnp.minimum(i, n_tiles - 1)
    ffn_tile = lambda i: jnp.maximum(i - 1, 0)
    mixer_out = lambda shape: pl.BlockSpec((1,) + shape, lambda i: (mixer_tile(i) // per_stream, 0, 0))
    return pl.pallas_call(
        functools.partial(_prompt_kernel, tiles_per_stream=per_stream),
        grid=(n_tiles + 1,),
        in_specs=[pl.BlockSpec((1, tile, D_MODEL),
                               lambda i: (mixer_tile(i) // per_stream, mixer_tile(i) % per_stream, 0))]
        + _weight_specs(),
        out_specs=[
            pl.BlockSpec((1, tile, D_MODEL), lambda i: (ffn_tile(i) // per_stream, ffn_tile(i) % per_stream, 0)),
            mixer_out((CONV_WIDTH - 1, D_CONV)),
            mixer_out((WINDOW, D_KV)),
            mixer_out((WINDOW, D_KV)),
            pl.BlockSpec((1, FFN_CONV_WIDTH - 1, D_FF), lambda i: (ffn_tile(i) // per_stream, 0, 0)),
        ],
        out_shape=[
            jax.ShapeDtypeStruct((batch, seq, D_MODEL), F32),
            jax.ShapeDtypeStruct((batch, CONV_WIDTH - 1, D_CONV), F32),
            jax.ShapeDtypeStruct((batch, WINDOW, D_KV), F32),
            jax.ShapeDtypeStruct((batch, WINDOW, D_KV), F32),
            jax.ShapeDtypeStruct((batch, FFN_CONV_WIDTH - 1, D_FF), F32),
        ],
        scratch_shapes=[
            pltpu.VMEM((CONV_PAD + tile, D_CONV), F32),
            _shift_scratch(CONV_ROWS),
            pltpu.VMEM((tile, D_ATTN), BF16),
            pltpu.VMEM((KV_PARTS, WINDOW + tile, LANES), BF16),
            pltpu.VMEM((KV_PARTS, WINDOW + tile, LANES), BF16),
            pltpu.VMEM((tile, D_ATTN), F32),
            pltpu.VMEM((tile, D_MIX), BF16),
            pltpu.VMEM((tile, D_MODEL), F32),
            pltpu.VMEM((1, FFN_PAD + tile, D_FF), F32),
            pltpu.VMEM((2, FFN_CONV_WIDTH - 1, 1, tile, FFN_COLS), F32),
            pltpu.VMEM((2, tile, FFN_COLS), F32),
            pltpu.VMEM((tile, D_FF), BF16),
        ],
        compiler_params=pltpu.CompilerParams(
            dimension_semantics=("arbitrary",),
            vmem_limit_bytes=VMEM_LIMIT_BYTES),
        name="prompt_layer",
    )(x, *weights)


def _sample_call(x, cconv, ck, cv, cffn, weights):
    n_streams, seq, _ = x.shape
    sb = SAMPLE_STREAMS
    rows = sb * seq
    per_stream = lambda shape: pl.BlockSpec((sb,) + shape, lambda i: (i, 0, 0))
    return pl.pallas_call(
        _sample_kernel,
        grid=(n_streams // sb,),
        in_specs=[
            pl.BlockSpec((rows, D_MODEL), lambda i: (i, 0)),
            per_stream((CONV_PAD, D_CONV)),
            per_stream((WINDOW, D_KV)),
            per_stream((WINDOW, D_KV)),
            per_stream((FFN_PAD, D_FF)),
        ] + _weight_specs(),
        out_specs=[
            pl.BlockSpec((rows, D_MODEL), lambda i: (i, 0)),
            per_stream((CONV_WIDTH - 1, D_CONV)),
            per_stream((WINDOW, D_KV)),
            per_stream((WINDOW, D_KV)),
            per_stream((FFN_CONV_WIDTH - 1, D_FF)),
        ],
        out_shape=[
            jax.ShapeDtypeStruct((n_streams * seq, D_MODEL), F32),
            jax.ShapeDtypeStruct((n_streams, CONV_WIDTH - 1, D_CONV), F32),
            jax.ShapeDtypeStruct((n_streams, WINDOW, D_KV), F32),
            jax.ShapeDtypeStruct((n_streams, WINDOW, D_KV), F32),
            jax.ShapeDtypeStruct((n_streams, FFN_CONV_WIDTH - 1, D_FF), F32),
        ],
        scratch_shapes=[
            pltpu.VMEM((sb, CONV_PAD + seq, D_CONV), F32),
            _shift_scratch(seq),
            pltpu.VMEM((rows, D_ATTN), BF16),
            pltpu.VMEM((KV_PARTS, sb, KEY_BLOCK, LANES), BF16),
            pltpu.VMEM((KV_PARTS, sb, KEY_BLOCK, LANES), BF16),
            pltpu.VMEM((sb, KV_PARTS, 2 * seq, KEY_BLOCK), F32),
            pltpu.VMEM((sb, KV_PARTS, 2 * seq, KEY_BLOCK), BF16),
            pltpu.VMEM((rows, D_ATTN), F32),
            pltpu.VMEM((rows, D_MIX), BF16),
            pltpu.VMEM((rows, D_MODEL), F32),
            pltpu.VMEM((sb, FFN_PAD + seq, D_FF), F32),
            pltpu.VMEM((2, FFN_CONV_WIDTH - 1, sb, seq, FFN_COLS), F32),
            pltpu.VMEM((2, rows, FFN_COLS), F32),
            pltpu.VMEM((rows, D_FF), BF16),
        ],
        compiler_params=pltpu.CompilerParams(
            dimension_semantics=("arbitrary",),
            vmem_limit_bytes=VMEM_LIMIT_BYTES),
        name="sample_layer",
    )(x.reshape(n_streams * seq, D_MODEL), cconv, ck, cv, cffn, *weights)


def kernel(x_prompt, x_sample, cache_conv, cache_k, cache_v, cache_ffn_conv, g_attn_norm, w_in, conv_w, conv_b,
           conv_ln_g, conv_ln_b, sinks, g_out_conv, g_out_attn, w_out, g_ffn_norm, w_up, ffn_conv_w, ffn_conv_b,
           w_down, g_final):
    depth = w_in.shape[0]
    assert depth == 1, "single trunk layer"
    batch, seq, _ = x_prompt.shape
    n_streams, dec_seq, _ = x_sample.shape
    assert seq % PROMPT_TILE == 0 and PROMPT_TILE % QUERY_BLOCK == 0 and PROMPT_TILE % CONV_ROWS == 0
    assert n_streams % SAMPLE_STREAMS == 0 and dec_seq % (2 * SUBLANES) == 0
    assert CONV_WIDTH - 1 <= dec_seq <= KEY_BLOCK - WINDOW

    row = lambda a: a.reshape(1, -1)
    weights = (
        row(g_attn_norm[0]), w_in[0].astype(BF16), conv_w[0], row(conv_b[0]), row(conv_ln_g[0]),
        row(conv_ln_b[0]), sinks[0], row(g_out_conv[0]), row(g_out_attn[0]), w_out[0].astype(BF16),
        row(g_ffn_norm[0]), w_up[0].astype(BF16), ffn_conv_w[0], row(ffn_conv_b[0]), w_down[0].astype(BF16),
        row(g_final),
    )

    y_p, conv_p, k_p, v_p, ffn_p = _prompt_call(x_prompt, weights)

    cconv = jnp.pad(cache_conv[0], ((0, 0), (CONV_PAD - (CONV_WIDTH - 1), 0), (0, 0)))
    cffn = jnp.pad(cache_ffn_conv[0], ((0, 0), (FFN_PAD - (FFN_CONV_WIDTH - 1), 0), (0, 0)))
    ck = cache_k[0].reshape(n_streams, WINDOW, D_KV)
    cv = cache_v[0].reshape(n_streams, WINDOW, D_KV)
    y_s, conv_s, k_s, v_s, ffn_s = _sample_call(x_sample, cconv, ck, cv, cffn, weights)

    heads = lambda a: a.reshape(depth, a.shape[0], WINDOW, N_KV_HEADS, HEAD_DIM)
    return (y_p, y_s.reshape(n_streams, dec_seq, D_MODEL),
            conv_p[None], heads(k_p), heads(v_p), ffn_p[None],
            conv_s[None], heads(k_s), heads(v_s), ffn_s[None])
```
